```python
import jax
import jax.numpy as jnp
from jax import lax
import numpy as np

D_MODEL = 4096
BATCH = 32
SEQ = 256
DEPTH = 4
DEC_BATCH = 4
DEC_SEQ = 4096
PAST_LEN = 256

GRID_W = 64
HEAD_DIM = 128
FNET_GROUPS = 4
FNET_GROUP_W = D_MODEL // 16
FNET_W = FNET_GROUPS * FNET_GROUP_W
WIN_Q_HEADS = D_MODEL // 256
WIN_KV_HEADS = WIN_Q_HEADS // 4
WIN = 128
WIN_BLOCK = 128
WIN_Q_W = WIN_Q_HEADS * HEAD_DIM
WIN_KV_W = WIN_KV_HEADS * HEAD_DIM
NA_HEADS = D_MODEL // 512
NA_KH = 8
NA_KW = 16
NA_W = NA_HEADS * HEAD_DIM
MLP_HIDDEN = 4 * D_MODEL
ROPE_BASE = 10000.0
EPS = 1e-6
NEG_INF = -1e30
A_END = FNET_W
BQ_END = A_END + WIN_Q_W
BK_END = BQ_END + WIN_KV_W
BV_END = BK_END + WIN_KV_W
CQ_END = BV_END + NA_W
CK_END = CQ_END + NA_W
IN_W = CK_END + NA_W

kernel_name = 'hybrid_dit_fnet_swa_natten_step'


def rmsnorm(x, g):
    x32 = x.astype(jnp.float32)
    y = x32 * lax.rsqrt(jnp.mean(x32 * x32, axis=-1, keepdims=True) + EPS)
    return (y * g.astype(jnp.float32)).astype(x.dtype)


def adaln(cvec, w_mod, b_mod):
    m = jax.nn.silu(cvec) @ w_mod + b_mod
    return jnp.split(m, 6, axis=-1)


def project_in(h, w_in):
    b, l = h.shape[0], h.shape[1]
    a, bq, bk, bv, cq, ck, cv = jnp.split(h @ w_in, [A_END, BQ_END, BK_END, BV_END, CQ_END, CK_END], axis=-1)
    def heads(t, n):
        return t.reshape(b, l, n, HEAD_DIM)
    return (a, heads(bq, WIN_Q_HEADS), heads(bk, WIN_KV_HEADS), heads(bv, WIN_KV_HEADS),
            heads(cq, NA_HEADS), heads(ck, NA_HEADS), heads(cv, NA_HEADS))


def fourier_mix(a):
    a4 = a.reshape(a.shape[0], a.shape[1], FNET_GROUPS, FNET_GROUP_W).astype(jnp.float32)
    f = jnp.fft.fft2(a4, axes=(1, 3), norm='ortho').real
    return f.reshape(a.shape).astype(a.dtype)


def axial_rope(x):
    s = x.shape[1]
    t = jnp.arange(s)
    n_freq = HEAD_DIM // 4
    inv = ROPE_BASE ** (-jnp.arange(n_freq, dtype=jnp.float32) / n_freq)
    def rot(xh, pos):
        ang = pos.astype(jnp.float32)[:, None] * inv
        cos = jnp.cos(ang)[:, None, :]
        sin = jnp.sin(ang)[:, None, :]
        x1, x2 = jnp.split(xh.astype(jnp.float32), 2, axis=-1)
        return jnp.concatenate([x1 * cos - x2 * sin, x2 * cos + x1 * sin], axis=-1)
    half = HEAD_DIM // 2
    out = jnp.concatenate([rot(x[..., :half], t // GRID_W), rot(x[..., half:], t % GRID_W)], axis=-1)
    return out.astype(x.dtype)


def ctx_attention(q, k, v, sink):
    b, l, hq, d = q.shape
    hk = k.shape[2]
    g = hq // hk
    qg = q.reshape(b, l, hk, g, d)
    s = jnp.einsum('bqhgd,bkhd->bhgqk', qg, k, preferred_element_type=jnp.float32) * (d ** -0.5)
    if sink is not None:
        sk = jnp.broadcast_to(sink.astype(jnp.float32).reshape(hk, g)[None, :, :, None, None], s.shape[:-1] + (1,))
        p = jax.nn.softmax(jnp.concatenate([s, sk], axis=-1), axis=-1)[..., :l]
    else:
        p = jax.nn.softmax(s, axis=-1)
    o = jnp.einsum('bhgqk,bkhd->bqhgd', p.astype(v.dtype), v)
    return o.reshape(b, l, hq * d)


def window_attention(q, k, v, kc, vc, sink):
    b, s_len, hq, d = q.shape
    hk = k.shape[2]
    g = hq // hk
    nb = s_len // WIN_BLOCK
    lc = kc.shape[1]
    nk = 3 * WIN_BLOCK
    qb = q.reshape(b, nb, WIN_BLOCK, hk, g, d)
    def band(t):
        tp = jnp.pad(t, ((0, 0), (WIN_BLOCK, WIN_BLOCK), (0, 0), (0, 0))).reshape(b, nb + 2, WIN_BLOCK, hk, d)
        return jnp.concatenate([tp[:, :-2], tp[:, 1:-1], tp[:, 2:]], axis=2)
    kb, vb = band(k), band(v)
    blk = jnp.arange(nb)[:, None, None]
    ipos = blk * WIN_BLOCK + jnp.arange(WIN_BLOCK)[None, :, None]
    jpos = (blk - 1) * WIN_BLOCK + jnp.arange(nk)[None, None, :]
    mask = (jnp.abs(ipos - jpos) <= WIN) & (jpos >= 0) & (jpos < s_len)
    scale = d ** -0.5
    sw = jnp.einsum('bnqhgd,bnkhd->bnhgqk', qb, kb, preferred_element_type=jnp.float32) * scale
    sw = jnp.where(mask[None, :, None, None], sw, NEG_INF)
    sc = jnp.einsum('bnqhgd,bchd->bnhgqc', qb, kc, preferred_element_type=jnp.float32) * scale
    sk = jnp.broadcast_to(sink.astype(jnp.float32).reshape(hk, g)[None, None, :, :, None, None], sw.shape[:-1] + (1,))
    p = jax.nn.softmax(jnp.concatenate([sw, sc, sk], axis=-1), axis=-1).astype(v.dtype)
    o = (jnp.einsum('bnhgqk,bnkhd->bnqhgd', p[..., :nk], vb)
         + jnp.einsum('bnhgqc,bchd->bnqhgd', p[..., nk:nk + lc], vc))
    return o.reshape(b, s_len, hq * d)


def neighbourhood_attention(q, k, v, kc, vc, rpb):
    b, s_len, h, d = q.shape
    rows_n = s_len // GRID_W
    kh = min(NA_KH, rows_n)
    kw = NA_KW
    lc = kc.shape[1]
    qg = q.reshape(b, rows_n, GRID_W, h, d)
    kg = k.reshape(b, rows_n, GRID_W, h, d)
    vg = v.reshape(b, rows_n, GRID_W, h, d)
    r = jnp.arange(rows_n)
    rs = jnp.clip(r - kh // 2, 0, rows_n - kh)
    rows = rs[:, None] + jnp.arange(kh)[None, :]
    kr = kg[:, rows]
    vr = vg[:, rows]
    col = jnp.arange(GRID_W)
    cs = jnp.clip(col - kw // 2, 0, GRID_W - kw)
    col_ok = (col[None, :] >= cs[:, None]) & (col[None, :] < cs[:, None] + kw)
    scale = d ** -0.5
    sw = jnp.einsum('brqhd,brikhd->bhrqik', qg, kr, preferred_element_type=jnp.float32) * scale
    idx_r = (rows - r[:, None] + NA_KH - 1)[:, None, :, None]
    idx_c = (jnp.clip(col[None, :] - col[:, None], -(NA_KW - 1), NA_KW - 1) + NA_KW - 1)[None, :, None, :]
    bias = rpb.astype(jnp.float32)[:, idx_r, idx_c]
    sw = jnp.where(col_ok[None, None, None, :, None, :], sw + bias[None], NEG_INF)
    sw = sw.reshape(b, h, rows_n, GRID_W, kh * GRID_W)
    sc = jnp.einsum('brqhd,bchd->bhrqc', qg, kc, preferred_element_type=jnp.float32) * scale
    p = jax.nn.softmax(jnp.concatenate([sw, sc], axis=-1), axis=-1).astype(v.dtype)
    pw = p[..., :kh * GRID_W].reshape(b, h, rows_n, GRID_W, kh, GRID_W)
    pc = p[..., kh * GRID_W:]
    o = (jnp.einsum('bhrqik,brikhd->brqhd', pw, vr)
         + jnp.einsum('bhrqc,bchd->brqhd', pc, vc))
    return o.reshape(b, s_len, h * d)


def merge_out(h, ya, yb, yc, w_a, w_b, w_c, w_gate, b_gate, w_out):
    gates = jax.nn.sigmoid((h @ w_gate + b_gate).astype(jnp.float32)).astype(h.dtype)
    ga, gb, gc = jnp.split(gates, 3, axis=-1)
    m = ga * (ya @ w_a) + gb * (yb @ w_b) + gc * (yc @ w_c)
    return m @ w_out


def sq_relu_mlp(h, w1, w2):
    return jnp.square(jax.nn.relu(h @ w1)) @ w2


def _normal(k, shape, scale):
    return scale * jax.random.normal(k, shape, jnp.float32)


def setup_inputs(seed: int = 0) -> dict:
    key = jax.random.key(seed)
    ks = jax.random.split(key, 26)
    dm = D_MODEL
    return {
        'x_prompt': _normal(ks[0], (BATCH, SEQ, dm), 1.0),
        'x_sample': _normal(ks[1], (DEC_BATCH, DEC_SEQ, dm), 1.0),
        'cache_win_k': _normal(ks[2], (DEC_BATCH, DEPTH, PAST_LEN, WIN_KV_HEADS, HEAD_DIM), 1.0),
        'cache_win_v': _normal(ks[3], (DEC_BATCH, DEPTH, PAST_LEN, WIN_KV_HEADS, HEAD_DIM), 1.0),
        'cache_na_k': _normal(ks[4], (DEC_BATCH, DEPTH, PAST_LEN, NA_HEADS, HEAD_DIM), 1.0),
        'cache_na_v': _normal(ks[5], (DEC_BATCH, DEPTH, PAST_LEN, NA_HEADS, HEAD_DIM), 1.0),
        'c': _normal(ks[6], (DEC_BATCH, dm), 1.0),
        'c_ctx': _normal(ks[7], (dm,), 1.0),
        'w_mod': _normal(ks[8], (DEPTH, dm, 6 * dm), 0.5 * dm ** -0.5),
        'b_mod': _normal(ks[9], (DEPTH, 6 * dm), 0.02),
        'norm1_g': 1.0 + _normal(ks[10], (DEPTH, dm), 0.02),
        'w_in': _normal(ks[11], (DEPTH, dm, IN_W), dm ** -0.5),
        'win_sink': _normal(ks[12], (DEPTH, WIN_Q_HEADS), 1.0),
        'na_rpb': _normal(ks[13], (DEPTH, NA_HEADS, 2 * NA_KH - 1, 2 * NA_KW - 1), 0.1),
        'w_a': _normal(ks[14], (DEPTH, FNET_W, dm), FNET_W ** -0.5),
        'w_b': _normal(ks[15], (DEPTH, WIN_Q_W, dm), WIN_Q_W ** -0.5),
        'w_c': _normal(ks[16], (DEPTH, NA_W, dm), NA_W ** -0.5),
        'w_gate': _normal(ks[17], (DEPTH, dm, 3 * dm), dm ** -0.5),
        'b_gate': _normal(ks[18], (DEPTH, 3 * dm), 0.02),
        'w_out': _normal(ks[19], (DEPTH, dm, dm), dm ** -0.5),
        'norm2_g': 1.0 + _normal(ks[20], (DEPTH, dm), 0.02),
        'w1': _normal(ks[21], (DEPTH, dm, MLP_HIDDEN), dm ** -0.5),
        'w2': _normal(ks[22], (DEPTH, MLP_HIDDEN, dm), MLP_HIDDEN ** -0.5),
        'final_g': 1.0 + _normal(ks[23], (dm,), 0.02),
    }


def reference(x_prompt, x_sample, cache_win_k, cache_win_v, cache_na_k, cache_na_v, c, c_ctx,
              w_mod, b_mod, norm1_g, w_in, win_sink, na_rpb, w_a, w_b, w_c, w_gate, b_gate,
              w_out, norm2_g, w1, w2, final_g):
    xp = x_prompt
    xs = x_sample
    wk_list, wv_list, nk_list, nv_list = [], [], [], []
    for l in range(DEPTH):
        sh1, sc1, gt1, sh2, sc2, gt2 = adaln(c_ctx, w_mod[l], b_mod[l])
        h = rmsnorm(xp, norm1_g[l]) * (1.0 + sc1) + sh1
        a, bq, bk, bv, cq, ck, cv = project_in(h, w_in[l])
        ya = fourier_mix(a)
        yb = ctx_attention(bq, bk, bv, win_sink[l])
        yc = ctx_attention(cq, ck, cv, None)
        xp = xp + gt1 * merge_out(h, ya, yb, yc, w_a[l], w_b[l], w_c[l], w_gate[l], b_gate[l], w_out[l])
        h = rmsnorm(xp, norm2_g[l]) * (1.0 + sc2) + sh2
        xp = xp + gt2 * sq_relu_mlp(h, w1[l], w2[l])
        wk_list.append(bk)
        wv_list.append(bv)
        nk_list.append(ck)
        nv_list.append(cv)
        sh1, sc1, gt1, sh2, sc2, gt2 = adaln(c[:, None, :], w_mod[l], b_mod[l])
        h = rmsnorm(xs, norm1_g[l]) * (1.0 + sc1) + sh1
        a, bq, bk, bv, cq, ck, cv = project_in(h, w_in[l])
        bq = axial_rope(bq)
        bk = axial_rope(bk)
        ya = fourier_mix(a)
        yb = window_attention(bq, bk, bv, cache_win_k[:, l], cache_win_v[:, l], win_sink[l])
        yc = neighbourhood_attention(cq, ck, cv, cache_na_k[:, l], cache_na_v[:, l], na_rpb[l])
        xs = xs + gt1 * merge_out(h, ya, yb, yc, w_a[l], w_b[l], w_c[l], w_gate[l], b_gate[l], w_out[l])
        h = rmsnorm(xs, norm2_g[l]) * (1.0 + sc2) + sh2
        xs = xs + gt2 * sq_relu_mlp(h, w1[l], w2[l])
    y_prompt = rmsnorm(xp, final_g)
    y_sample = rmsnorm(xs, final_g)
    new_win_k = jnp.stack(wk_list, axis=1)
    new_win_v = jnp.stack(wv_list, axis=1)
    new_na_k = jnp.stack(nk_list, axis=1)
    new_na_v = jnp.stack(nv_list, axis=1)
    return (y_prompt, y_sample, new_win_k, new_win_v, new_na_k, new_na_v)
```

```python
import functools
import math

import jax
import jax.numpy as jnp
from jax import lax
from jax.experimental import pallas as pl
from jax.experimental.pallas import tpu as pltpu

F32 = jnp.float32
BF16 = jnp.bfloat16

D_MODEL = 4096
DEPTH = 4
GRID_W = 64
HEAD_DIM = 128
FNET_GROUP_W = 256
FNET_W = 1024
WIN_Q_HEADS = 16
WIN_KV_HEADS = 4
WIN_GROUP = WIN_Q_HEADS // WIN_KV_HEADS
WIN = 128
WIN_Q_W = WIN_Q_HEADS * HEAD_DIM
WIN_KV_W = WIN_KV_HEADS * HEAD_DIM
NA_HEADS = 8
NA_KH = 8
NA_KW = 16
NA_W = NA_HEADS * HEAD_DIM
MLP_HIDDEN = 4 * D_MODEL
IN_W = 7168
ROPE_BASE = 10000.0
EPS = 1e-6
NEG_INF = -1e30
ATTN_SCALE = HEAD_DIM ** -0.5
N_MOD = 8

COL_BQ, COL_A, COL_BK, COL_BV, COL_CQ, COL_CK, COL_CV = 0, 2048, 3072, 3584, 4096, 5120, 6144

V7X_SCOPED_VMEM_BYTES = 60000 * 1024
VMEM_MARGIN_BYTES = 6 << 20


def _cparams(semantics, est_bytes):
    limit = min(V7X_SCOPED_VMEM_BYTES, int(est_bytes) + VMEM_MARGIN_BYTES)
    return pltpu.CompilerParams(dimension_semantics=semantics, vmem_limit_bytes=limit)


def _nbytes(shape, dtype):
    return math.prod(shape) * jnp.dtype(dtype).itemsize


def _adaln_kernel(c_ref, w_ref, b_ref, o_ref):
    c = c_ref[...]
    s = (c * (1.0 / (1.0 + jnp.exp(-c)))).astype(BF16)
    o_ref[...] = jnp.dot(s, w_ref[...].astype(BF16), preferred_element_type=F32) + b_ref[...]


def adaln_all(cvec, w_mod, b_mod, tn=512):
    depth, d, n = w_mod.shape
    est = 2 * _nbytes((d, tn), F32) + _nbytes((d, tn), BF16) + 4 * _nbytes((N_MOD, d), F32)
    return pl.pallas_call(
        _adaln_kernel,
        grid=(depth, n // tn),
        in_specs=[
            pl.BlockSpec((N_MOD, d), lambda l, j: (0, 0)),
            pl.BlockSpec((None, d, tn), lambda l, j: (l, 0, j)),
            pl.BlockSpec((None, 1, tn), lambda l, j: (l, 0, j)),
        ],
        out_specs=pl.BlockSpec((None, N_MOD, tn), lambda l, j: (l, 0, j)),
        out_shape=jax.ShapeDtypeStruct((depth, N_MOD, n), F32),
        compiler_params=_cparams(("arbitrary", "arbitrary"), est),
        name="adaln",
    )(cvec, w_mod, b_mod.reshape(depth, 1, n))


def _norm_kernel(*refs, modulated):
    if modulated:
        x_ref, g_ref, sc_ref, sh_ref, o_ref = refs
    else:
        x_ref, g_ref, o_ref = refs
    x = x_ref[...]
    y = x * lax.rsqrt(jnp.mean(x * x, axis=-1, keepdims=True) + EPS)
    y = y * g_ref[...]
    if modulated:
        y = y * (1.0 + sc_ref[...]) + sh_ref[...]
    o_ref[...] = y.astype(o_ref.dtype)


def norm_mod(x, g, sc=None, sh=None, *, rows_per_mod=None, out_dtype=BF16, tr=256):
    m, d = x.shape
    modulated = sc is not None
    in_specs = [pl.BlockSpec((tr, d), lambda i: (i, 0)), pl.BlockSpec((1, d), lambda i: (0, 0))]
    args = [x, g.reshape(1, d)]
    if modulated:
        mod_spec = pl.BlockSpec((None, 1, d), lambda i: ((i * tr) // rows_per_mod, 0, 0))
        in_specs += [mod_spec, mod_spec]
        args += [sc, sh]
    est = 2 * _nbytes((tr, d), F32) + 2 * _nbytes((tr, d), out_dtype) + 3 * _nbytes((tr, d), F32)
    return pl.pallas_call(
        functools.partial(_norm_kernel, modulated=modulated),
        grid=(m // tr,),
        in_specs=in_specs,
        out_specs=pl.BlockSpec((tr, d), lambda i: (i, 0)),
        out_shape=jax.ShapeDtypeStruct((m, d), out_dtype),
        compiler_params=_cparams(("arbitrary",), est),
        name="norm_mod",
    )(*args)


def _epilogue(acc, epilogue, r_ref, gate_ref, o_ref):
    if epilogue == "relu2":
        a = jnp.maximum(acc, 0.0)
        o_ref[...] = (a * a).astype(o_ref.dtype)
    elif epilogue == "resid":
        o_ref[...] = (r_ref[...] + gate_ref[...] * acc).astype(o_ref.dtype)
    else:
        o_ref[...] = acc.astype(o_ref.dtype)


def _mm_kernel(*refs, epilogue, nk):
    if epilogue == "resid":
        x_ref, w_ref, r_ref, gate_ref = refs[:4]
        rest = refs[4:]
    else:
        x_ref, w_ref = refs[:2]
        r_ref = gate_ref = None
        rest = refs[2:]
    o_ref = rest[0]
    part = jnp.dot(x_ref[...], w_ref[...], preferred_element_type=F32)
    if nk == 1:
        _epilogue(part, epilogue, r_ref, gate_ref, o_ref)
        return
    acc_ref = rest[1]
    k = pl.program_id(2)

    @pl.when(k == 0)
    def _():
        acc_ref[...] = part

    @pl.when(k > 0)
    def _():
        acc_ref[...] += part

    @pl.when(k == nk - 1)
    def _():
        _epilogue(acc_ref[...], epilogue, r_ref, gate_ref, o_ref)


def matmul(x, w, *, tm, tn, tk=None, epilogue="plain", out_dtype=F32, resid=None, gate=None,
           rows_per_mod=None, out_rows=None, out_index=None):
    m, kdim = x.shape
    n = w.shape[1]
    tk = kdim if tk is None else tk
    nk = kdim // tk
    grid = (m // tm, n // tn, nk)
    in_specs = [pl.BlockSpec((tm, tk), lambda i, j, k: (i, k)),
                pl.BlockSpec((tk, tn), lambda i, j, k: (k, j))]
    args = [x, w]
    est = 2 * _nbytes((tm, tk), x.dtype) + 2 * _nbytes((tk, tn), w.dtype)
    est += 2 * _nbytes((tm, tn), out_dtype) + 2 * _nbytes((tm, tn), F32)
    if epilogue == "resid":
        in_specs += [pl.BlockSpec((tm, tn), lambda i, j, k: (i, j)),
                     pl.BlockSpec((None, 1, tn), lambda i, j, k: ((i * tm) // rows_per_mod, 0, j))]
        args += [resid, gate]
        est += 2 * _nbytes((tm, tn), F32)
    scratch = [pltpu.VMEM((tm, tn), F32)] if nk > 1 else []
    if out_index is None:
        out_index = lambda i, j, k: (i, j)
        out_shape = (m, n)
    else:
        out_shape = out_rows
    return pl.pallas_call(
        functools.partial(_mm_kernel, epilogue=epilogue, nk=nk),
        grid=grid,
        in_specs=in_specs,
        out_specs=pl.BlockSpec((tm, tn), out_index),
        out_shape=jax.ShapeDtypeStruct(out_shape, out_dtype),
        scratch_shapes=scratch,
        compiler_params=_cparams(("arbitrary", "arbitrary", "arbitrary"), est),
        name="mm_" + epilogue,
    )(*args)


def _sigmoid(z):
    return 1.0 / (1.0 + jnp.exp(-z))


def _merge_kernel(h_ref, ya_ref, yb_ref, yc_ref, wga_ref, wgb_ref, wgc_ref, bga_ref, bgb_ref, bgc_ref,
                  wa_ref, wb_ref, wc_ref, o_ref):
    h = h_ref[...]

    def branch(wg_ref, bg_ref, y_ref, w_ref):
        gate = _sigmoid(jnp.dot(h, wg_ref[...], preferred_element_type=F32) + bg_ref[...])
        return gate * jnp.dot(y_ref[...], w_ref[...], preferred_element_type=F32)

    m = branch(wga_ref, bga_ref, ya_ref, wa_ref)
    m = m + branch(wgb_ref, bgb_ref, yb_ref, wb_ref)
    m = m + branch(wgc_ref, bgc_ref, yc_ref, wc_ref)
    o_ref[...] = m.astype(o_ref.dtype)


def merge(h, ya, yb, yc, w_gate, b_gate, w_a, w_b, w_c, *, tm=512, tn=256):
    m, d = h.shape
    nj = d // tn
    b_gate = b_gate.reshape(1, 3 * d)
    row = lambda i, j: (i, 0)
    in_specs = [
        pl.BlockSpec((tm, d), row),
        pl.BlockSpec((tm, ya.shape[1]), row),
        pl.BlockSpec((tm, yb.shape[1]), row),
        pl.BlockSpec((tm, yc.shape[1]), row),
        pl.BlockSpec((d, tn), lambda i, j: (0, j)),
        pl.BlockSpec((d, tn), lambda i, j: (0, nj + j)),
        pl.BlockSpec((d, tn), lambda i, j: (0, 2 * nj + j)),
        pl.BlockSpec((1, tn), lambda i, j: (0, j)),
        pl.BlockSpec((1, tn), lambda i, j: (0, nj + j)),
        pl.BlockSpec((1, tn), lambda i, j: (0, 2 * nj + j)),
        pl.BlockSpec((w_a.shape[0], tn), lambda i, j: (0, j)),
        pl.BlockSpec((w_b.shape[0], tn), lambda i, j: (0, j)),
        pl.BlockSpec((w_c.shape[0], tn), lambda i, j: (0, j)),
    ]
    est = 2 * 2 * _nbytes((tm, d), BF16) + 2 * 4 * _nbytes((d, tn), BF16) + 8 * _nbytes((tm, tn), F32)
    return pl.pallas_call(
        _merge_kernel,
        grid=(m // tm, nj),
        in_specs=in_specs,
        out_specs=pl.BlockSpec((tm, tn), lambda i, j: (i, j)),
        out_shape=jax.ShapeDtypeStruct((m, d), BF16),
        compiler_params=_cparams(("arbitrary", "arbitrary"), est),
        name="merge",
    )(h, ya, yb, yc, w_gate, w_gate, w_gate, b_gate, b_gate, b_gate, w_a, w_b, w_c)


def _split_hi_lo(x):
    hi = x.astype(BF16)
    lo = (x - hi.astype(F32)).astype(BF16)
    return hi, lo


def _dot3(xh, xl, wh, wl):
    acc = jnp.dot(xh, wh, preferred_element_type=F32)
    acc += jnp.dot(xl, wh, preferred_element_type=F32)
    acc += jnp.dot(xh, wl, preferred_element_type=F32)
    return acc


def _dot_nt(a, b):
    return lax.dot_general(a, b, (((1,), (1,)), ((), ())), preferred_element_type=F32)


def _rope(x, cos, sin_lo, sin_hi):
    quarter = HEAD_DIM // 4
    return (x * cos + pltpu.roll(x, HEAD_DIM - quarter, 1) * sin_lo + pltpu.roll(x, quarter, 1) * sin_hi)


def _softmax_parts(scores, extra_max=None):
    m = scores[0].max(axis=-1, keepdims=True)
    for s in scores[1:]:
        m = jnp.maximum(m, s.max(axis=-1, keepdims=True))
    if extra_max is not None:
        m = jnp.maximum(m, extra_max)
    return m


def _ctx_mixer_kernel(bq_ref, a_ref, bk_ref, bv_ref, cq_ref, ck_ref, cv_ref, sink_ref,
                      ch_h_ref, ch_l_ref, pc_h_ref, pc_l_ref, ps_h_ref, ps_l_ref,
                      ya_ref, yb_ref, yc_ref, *, seq):
    gw = FNET_GROUP_W
    ch_h = ch_h_ref[...]
    ch_l = ch_l_ref[...]
    pc_h, pc_l, ps_h, ps_l = pc_h_ref[...], pc_l_ref[...], ps_h_ref[...], ps_l_ref[...]
    norm = 1.0 / math.sqrt(seq * gw)
    for g in range(FNET_W // gw):
        a_h, a_l = _split_hi_lo(a_ref[:, g * gw:(g + 1) * gw])
        t = _dot3(a_h, a_l, ch_h, ch_l)
        tc_h, tc_l = _split_hi_lo(t[:, :gw])
        ts_h, ts_l = _split_hi_lo(t[:, gw:])
        y = _dot3(pc_h, pc_l, tc_h, tc_l) - _dot3(ps_h, ps_l, ts_h, ts_l)
        ya_ref[:, g * gw:(g + 1) * gw] = (y * norm).astype(ya_ref.dtype)

    hd = HEAD_DIM
    row_group = lax.broadcasted_iota(jnp.int32, (WIN_GROUP * seq, 1), 0) // seq
    for hk in range(WIN_KV_HEADS):
        q = jnp.concatenate(
            [bq_ref[:, (hk * WIN_GROUP + g) * hd:(hk * WIN_GROUP + g + 1) * hd] for g in range(WIN_GROUP)],
            axis=0)
        q = (q * ATTN_SCALE).astype(BF16)
        k = bk_ref[:, hk * hd:(hk + 1) * hd].astype(BF16)
        v = bv_ref[:, hk * hd:(hk + 1) * hd].astype(BF16)
        s = _dot_nt(q, k)
        sink = jnp.zeros((WIN_GROUP * seq, 1), F32)
        for g in range(WIN_GROUP):
            sink = jnp.where(row_group == g, sink_ref[hk * WIN_GROUP + g], sink)
        m = _softmax_parts([s], sink)
        e = jnp.exp(s - m)
        den = e.sum(axis=-1, keepdims=True) + jnp.exp(sink - m)
        o = jnp.dot(e.astype(BF16), v, preferred_element_type=F32) / den
        for g in range(WIN_GROUP):
            h = hk * WIN_GROUP + g
            yb_ref[:, h * hd:(h + 1) * hd] = o[g * seq:(g + 1) * seq].astype(yb_ref.dtype)

    for h in range(NA_HEADS):
        q = (cq_ref[:, h * hd:(h + 1) * hd] * ATTN_SCALE).astype(BF16)
        k = ck_ref[:, h * hd:(h + 1) * hd].astype(BF16)
        v = cv_ref[:, h * hd:(h + 1) * hd].astype(BF16)
        s = _dot_nt(q, k)
        m = s.max(axis=-1, keepdims=True)
        e = jnp.exp(s - m)
        den = e.sum(axis=-1, keepdims=True)
        o = jnp.dot(e.astype(BF16), v, preferred_element_type=F32) / den
        yc_ref[:, h * hd:(h + 1) * hd] = o.astype(yc_ref.dtype)


def ctx_mixers(proj, sink, tabs, *, batch, seq):
    def col(width, offset):
        return pl.BlockSpec((seq, width), lambda b: (b, offset // width))

    const = lambda shape: pl.BlockSpec(shape, lambda b: (0,) * len(shape))
    in_specs = [
        col(WIN_Q_W, COL_BQ), col(FNET_W, COL_A), col(WIN_KV_W, COL_BK), col(WIN_KV_W, COL_BV),
        col(NA_W, COL_CQ), col(NA_W, COL_CK), col(NA_W, COL_CV),
        pl.BlockSpec(memory_space=pltpu.SMEM),
        const(tabs["ch_h"].shape), const(tabs["ch_l"].shape),
        const(tabs["pc_h"].shape), const(tabs["pc_l"].shape),
        const(tabs["ps_h"].shape), const(tabs["ps_l"].shape),
    ]
    out_specs = [pl.BlockSpec((seq, FNET_W), lambda b: (b, 0)),
                 pl.BlockSpec((seq, WIN_Q_W), lambda b: (b, 0)),
                 pl.BlockSpec((seq, NA_W), lambda b: (b, 0))]
    m = batch * seq
    out_shape = [jax.ShapeDtypeStruct((m, FNET_W), BF16), jax.ShapeDtypeStruct((m, WIN_Q_W), BF16),
                 jax.ShapeDtypeStruct((m, NA_W), BF16)]
    est = 2 * _nbytes((seq, IN_W), F32) + 2 * _nbytes((seq, D_MODEL), BF16) + (16 << 20)
    return pl.pallas_call(
        functools.partial(_ctx_mixer_kernel, seq=seq),
        grid=(batch,),
        in_specs=in_specs,
        out_specs=out_specs,
        out_shape=out_shape,
        compiler_params=_cparams(("arbitrary",), est),
        name="ctx_mixers",
    )(proj, proj, proj, proj, proj, proj, proj, sink,
      tabs["ch_h"], tabs["ch_l"], tabs["pc_h"], tabs["pc_l"], tabs["ps_h"], tabs["ps_l"])


def _dft_chan_kernel(a_ref, ch_h_ref, ch_l_ref, o_ref):
    gw = FNET_GROUP_W
    ch_h = ch_h_ref[...]
    ch_l = ch_l_ref[...]
    for g in range(FNET_W // gw):
        a_h, a_l = _split_hi_lo(a_ref[:, g * gw:(g + 1) * gw])
        t = _dot3(a_h, a_l, ch_h, ch_l)
        tc_h, tc_l = _split_hi_lo(t[:, :gw])
        ts_h, ts_l = _split_hi_lo(t[:, gw:])
        cols = slice(g * gw, (g + 1) * gw)
        o_ref[0, :, cols] = tc_h
        o_ref[1, :, cols] = ts_h
        o_ref[2, :, cols] = tc_h
        o_ref[3, :, cols] = ts_h
        o_ref[4, :, cols] = tc_l
        o_ref[5, :, cols] = ts_l


def dft_chan(proj, tabs, *, batch, seq, tm=512):
    nt = seq // tm
    est = 2 * _nbytes((tm, FNET_W), F32) + 2 * 6 * _nbytes((tm, FNET_W), BF16) + (8 << 20)
    return pl.pallas_call(
        _dft_chan_kernel,
        grid=(batch, nt),
        in_specs=[pl.BlockSpec((tm, FNET_W), lambda b, i: (b * nt + i, COL_A // FNET_W)),
                  pl.BlockSpec(tabs["ch_h"].shape, lambda b, i: (0, 0)),
                  pl.BlockSpec(tabs["ch_l"].shape, lambda b, i: (0, 0))],
        out_specs=pl.BlockSpec((6, tm, FNET_W), lambda b, i: (0, i, b)),
        out_shape=jax.ShapeDtypeStruct((6, seq, batch * FNET_W), BF16),
        compiler_params=_cparams(("arbitrary", "arbitrary"), est),
        name="dft_chan",
    )(proj, tabs["ch_h"], tabs["ch_l"])


def _win_attn_kernel(q_ref, kp_ref, kc_ref, kn_ref, vp_ref, vc_ref, vn_ref, ck_ref, cv_ref, sink_ref,
                     cos_ref, slo_ref, shi_ref, o_ref, *, nblk):
    n = pl.program_id(1)
    blk = WIN
    hd = HEAD_DIM
    starts = (jnp.maximum(n - 1, 0) * blk, n * blk, jnp.minimum(n + 1, nblk - 1) * blk)

    def tables(start):
        sl = pl.ds(pl.multiple_of(start, blk), blk)
        return cos_ref[sl, :], slo_ref[sl, :], shi_ref[sl, :]

    tabs = [tables(s) for s in starts]
    q_tab = tabs[1]

    rows = WIN_GROUP * blk
    ipos = n * blk + lax.broadcasted_iota(jnp.int32, (rows, 3 * blk), 0) % blk
    jpos = (n - 1) * blk + lax.broadcasted_iota(jnp.int32, (rows, 3 * blk), 1)
    valid = (jnp.abs(ipos - jpos) <= WIN) & (jpos >= 0) & (jpos < nblk * blk)
    row_group = lax.broadcasted_iota(jnp.int32, (rows, 1), 0) // blk

    for hk in range(WIN_KV_HEADS):
        hs = slice(hk * hd, (hk + 1) * hd)
        kb = jnp.concatenate([_rope(r[:, hs], *t) for r, t in zip((kp_ref, kc_ref, kn_ref), tabs)],
                             axis=0).astype(BF16)
        vb = jnp.concatenate([r[:, hs] for r in (vp_ref, vc_ref, vn_ref)], axis=0).astype(BF16)
        q = jnp.concatenate(
            [_rope(q_ref[:, (hk * WIN_GROUP + g) * hd:(hk * WIN_GROUP + g + 1) * hd], *q_tab)
             for g in range(WIN_GROUP)], axis=0)
        q = (q * ATTN_SCALE).astype(BF16)
        s = jnp.where(valid, _dot_nt(q, kb), NEG_INF)
        sc = _dot_nt(q, ck_ref[:, hs].astype(BF16))
        sink = jnp.zeros((rows, 1), F32)
        for g in range(WIN_GROUP):
            sink = jnp.where(row_group == g, sink_ref[hk * WIN_GROUP + g], sink)
        m = _softmax_parts([s, sc], sink)
        e = jnp.exp(s - m)
        ec = jnp.exp(sc - m)
        den = e.sum(axis=-1, keepdims=True) + ec.sum(axis=-1, keepdims=True) + jnp.exp(sink - m)
        o = jnp.dot(e.astype(BF16), vb, preferred_element_type=F32)
        o += jnp.dot(ec.astype(BF16), cv_ref[:, hs].astype(BF16), preferred_element_type=F32)
        o = o / den
        for g in range(WIN_GROUP):
            h = hk * WIN_GROUP + g
            o_ref[:, h * hd:(h + 1) * hd] = o[g * blk:(g + 1) * blk].astype(o_ref.dtype)


def win_attention(proj, cache_k, cache_v, sink, rope_tabs, *, batch, seq):
    blk = WIN
    nblk = seq // blk
    past = cache_k.shape[1]

    def kv_spec(offset, shift):
        def idx(b, n):
            return (b * nblk + jnp.clip(n + shift, 0, nblk - 1), offset // WIN_KV_W)
        return pl.BlockSpec((blk, WIN_KV_W), idx)

    tab_spec = pl.BlockSpec((seq, HEAD_DIM), lambda b, n: (0, 0))
    cache_spec = pl.BlockSpec((None, past, WIN_KV_W), lambda b, n: (b, 0, 0))
    in_specs = [
        pl.BlockSpec((blk, WIN_Q_W), lambda b, n: (b * nblk + n, COL_BQ // WIN_Q_W)),
        kv_spec(COL_BK, -1), kv_spec(COL_BK, 0), kv_spec(COL_BK, 1),
        kv_spec(COL_BV, -1), kv_spec(COL_BV, 0), kv_spec(COL_BV, 1),
        cache_spec, cache_spec,
        pl.BlockSpec(memory_space=pltpu.SMEM),
        tab_spec, tab_spec, tab_spec,
    ]
    est = 6 * _nbytes((seq, HEAD_DIM), F32) + (24 << 20)
    return pl.pallas_call(
        functools.partial(_win_attn_kernel, nblk=nblk),
        grid=(batch, nblk),
        in_specs=in_specs,
        out_specs=pl.BlockSpec((blk, WIN_Q_W), lambda b, n: (b * nblk + n, 0)),
        out_shape=jax.ShapeDtypeStruct((batch * seq, WIN_Q_W), BF16),
        compiler_params=_cparams(("arbitrary", "arbitrary"), est),
        name="win_attn",
    )(proj, proj, proj, proj, proj, proj, proj, cache_k, cache_v, sink,
      rope_tabs[0], rope_tabs[1], rope_tabs[2])


def _na_attn_kernel(*refs):
    q_ref = refs[0]
    k_refs = refs[1:1 + NA_KH]
    v_refs = refs[1 + NA_KH:1 + 2 * NA_KH]
    ck_ref, cv_ref, bias_ref, o_ref = refs[1 + 2 * NA_KH:]
    hd = HEAD_DIM
    for h in range(NA_HEADS):
        hs = slice(h * hd, (h + 1) * hd)
        q = (q_ref[:, hs] * ATTN_SCALE).astype(BF16)
        kw = jnp.concatenate([r[:, hs] for r in k_refs], axis=0).astype(BF16)
        vw = jnp.concatenate([r[:, hs] for r in v_refs], axis=0).astype(BF16)
        s = _dot_nt(q, kw) + bias_ref[h]
        sc = _dot_nt(q, ck_ref[:, hs].astype(BF16))
        m = _softmax_parts([s, sc])
        e = jnp.exp(s - m)
        ec = jnp.exp(sc - m)
        den = e.sum(axis=-1, keepdims=True) + ec.sum(axis=-1, keepdims=True)
        o = jnp.dot(e.astype(BF16), vw, preferred_element_type=F32)
        o += jnp.dot(ec.astype(BF16), cv_ref[:, hs].astype(BF16), preferred_element_type=F32)
        o_ref[:, hs] = (o / den).astype(o_ref.dtype)


def _na_row_start(r, rows_n):
    return jnp.clip(r - NA_KH // 2, 0, rows_n - NA_KH)


def na_attention(proj, cache_k, cache_v, bias_tab, *, batch, seq):
    rows_n = seq // GRID_W
    past = cache_k.shape[1]

    def kv_spec(offset, i):
        def idx(b, r):
            return (b * rows_n + _na_row_start(r, rows_n) + i, offset // NA_W)
        return pl.BlockSpec((GRID_W, NA_W), idx)

    cache_spec = pl.BlockSpec((None, past, NA_W), lambda b, r: (b, 0, 0))
    in_specs = [pl.BlockSpec((GRID_W, NA_W), lambda b, r: (b * rows_n + r, COL_CQ // NA_W))]
    in_specs += [kv_spec(COL_CK, i) for i in range(NA_KH)]
    in_specs += [kv_spec(COL_CV, i) for i in range(NA_KH)]
    in_specs += [cache_spec, cache_spec,
                 pl.BlockSpec((None, NA_HEADS, GRID_W, NA_KH * GRID_W),
                              lambda b, r: (_na_row_start(r, rows_n) - r + NA_KH - 1, 0, 0, 0))]
    est = 24 << 20
    return pl.pallas_call(
        _na_attn_kernel,
        grid=(batch, rows_n),
        in_specs=in_specs,
        out_specs=pl.BlockSpec((GRID_W, NA_W), lambda b, r: (b * rows_n + r, 0)),
        out_shape=jax.ShapeDtypeStruct((batch * seq, NA_W), BF16),
        compiler_params=_cparams(("arbitrary", "arbitrary"), est),
        name="na_attn",
    )(*([proj] * (1 + 2 * NA_KH)), cache_k, cache_v, bias_tab)


def _dft_cos_sin(n):
    idx = jnp.arange(n, dtype=jnp.int32)
    phase = (idx[:, None] * idx[None, :]) % n
    ang = phase.astype(F32) * (2.0 * math.pi / n)
    return jnp.cos(ang), jnp.sin(ang)


def _hi_lo(x):
    hi = x.astype(BF16)
    return hi, (x - hi.astype(F32)).astype(BF16)


def _fourier_tables(ctx_seq, lat_seq):
    cc, sc = _dft_cos_sin(FNET_GROUP_W)
    ch_h, ch_l = _hi_lo(jnp.concatenate([cc, sc], axis=1))
    cp, sp = _dft_cos_sin(ctx_seq)
    pc_h, pc_l = _hi_lo(cp)
    ps_h, ps_l = _hi_lo(sp)
    cl, sl = _dft_cos_sin(lat_seq)
    p_h, p_l = _hi_lo(jnp.concatenate([cl, -sl], axis=1))
    p3 = jnp.concatenate([p_h, p_l, p_h], axis=1)
    return dict(ch_h=ch_h, ch_l=ch_l, pc_h=pc_h, pc_l=pc_l, ps_h=ps_h, ps_l=ps_l, p3=p3)


def _rope_tables(seq):
    t = jnp.arange(seq)
    n_freq = HEAD_DIM // 4
    inv = ROPE_BASE ** (-jnp.arange(n_freq, dtype=F32) / n_freq)
    ang_r = (t // GRID_W).astype(F32)[:, None] * inv
    ang_c = (t % GRID_W).astype(F32)[:, None] * inv
    zeros = jnp.zeros_like(ang_r)
    cos = jnp.concatenate([jnp.cos(ang_r)] * 2 + [jnp.cos(ang_c)] * 2, axis=1)
    sin_lo = jnp.concatenate([-jnp.sin(ang_r), zeros, -jnp.sin(ang_c), zeros], axis=1)
    sin_hi = jnp.concatenate([zeros, jnp.sin(ang_r), zeros, jnp.sin(ang_c)], axis=1)
    return cos, sin_lo, sin_hi


def _na_bias_tables(rpb):
    col = jnp.arange(GRID_W)
    cs = jnp.clip(col - NA_KW // 2, 0, GRID_W - NA_KW)
    col_ok = (col[None, :] >= cs[:, None]) & (col[None, :] < cs[:, None] + NA_KW)
    idx_c = jnp.clip(col[None, :] - col[:, None], -(NA_KW - 1), NA_KW - 1) + NA_KW - 1
    bt = jnp.where(col_ok[None, None, None], rpb.astype(F32)[:, :, :, idx_c], NEG_INF)
    tabs = []
    for r0 in range(NA_KH):
        win = bt[:, :, r0:r0 + NA_KH]
        tabs.append(jnp.moveaxis(win, 2, 3).reshape(win.shape[0], NA_HEADS, GRID_W, NA_KH * GRID_W))
    return jnp.stack(tabs, axis=1)


def _permute_in_cols(w_in):
    a, bq, rest = w_in[..., :FNET_W], w_in[..., FNET_W:FNET_W + WIN_Q_W], w_in[..., FNET_W + WIN_Q_W:]
    return jnp.concatenate([bq, a, rest], axis=-1)


def kernel(x_prompt, x_sample, cache_win_k, cache_win_v, cache_na_k, cache_na_v, c, c_ctx, w_mod, b_mod,
           norm1_g, w_in, win_sink, na_rpb, w_a, w_b, w_c, w_gate, b_gate, w_out, norm2_g, w1, w2, final_g):
    batch, seq, d = x_prompt.shape
    dec_batch, dec_seq, _ = x_sample.shape
    past = cache_win_k.shape[2]
    m_ctx, m_lat = batch * seq, dec_batch * dec_seq

    w_in_b = _permute_in_cols(w_in).astype(BF16)
    w_a_b, w_b_b, w_c_b = w_a.astype(BF16), w_b.astype(BF16), w_c.astype(BF16)
    w_gate_b, w_out_b = w_gate.astype(BF16), w_out.astype(BF16)
    w1_b, w2_b = w1.astype(BF16), w2.astype(BF16)

    cvec = jnp.concatenate([c_ctx[None], c, jnp.zeros((N_MOD - 1 - dec_batch, d), F32)], axis=0)
    mods = adaln_all(cvec, w_mod, b_mod).reshape(DEPTH, N_MOD, 6, 1, d)

    ftabs = _fourier_tables(seq, dec_seq)
    rope_tabs = _rope_tables(dec_seq)
    na_bias = _na_bias_tables(na_rpb)

    xp = x_prompt.reshape(m_ctx, d)
    xs = x_sample.reshape(m_lat, d)
    groups = (
        dict(rows_per_mod=m_ctx, mod=slice(0, 1)),
        dict(rows_per_mod=dec_seq, mod=slice(1, 1 + dec_batch)),
    )
    wk_list, wv_list, nk_list, nv_list = [], [], [], []
    for l in range(DEPTH):
        new_x = []
        for gi, (x, grp) in enumerate(zip((xp, xs), groups)):
            rpm = grp["rows_per_mod"]
            sh1, sc1, gt1, sh2, sc2, gt2 = [mods[l, grp["mod"], k] for k in range(6)]
            h = norm_mod(x, norm1_g[l], sc1, sh1, rows_per_mod=rpm)
            proj = matmul(h, w_in_b[l], tm=1024, tn=1024)
            if gi == 0:
                ya, yb, yc = ctx_mixers(proj, win_sink[l], ftabs, batch=batch, seq=seq)
                wk_list.append(proj[:, COL_BK:COL_BK + WIN_KV_W])
                wv_list.append(proj[:, COL_BV:COL_BV + WIN_KV_W])
                nk_list.append(proj[:, COL_CK:COL_CK + NA_W])
                nv_list.append(proj[:, COL_CV:COL_CV + NA_W])
            else:
                slabs = dft_chan(proj, ftabs, batch=dec_batch, seq=dec_seq)
                nrow = dec_seq // 1024
                ya = matmul(ftabs["p3"], slabs.reshape(6 * dec_seq, dec_batch * FNET_W),
                            tm=1024, tn=FNET_W, tk=2048, out_dtype=F32,
                            out_rows=(m_lat, FNET_W), out_index=lambda i, j, k: (j * nrow + i, 0))
                ya = (ya * (1.0 / math.sqrt(dec_seq * FNET_GROUP_W))).astype(BF16)
                yb = win_attention(proj, cache_win_k[:, l].reshape(dec_batch, past, WIN_KV_W),
                                   cache_win_v[:, l].reshape(dec_batch, past, WIN_KV_W),
                                   win_sink[l], rope_tabs, batch=dec_batch, seq=dec_seq)
                yc = na_attention(proj, cache_na_k[:, l].reshape(dec_batch, past, NA_W),
                                  cache_na_v[:, l].reshape(dec_batch, past, NA_W),
                                  na_bias[l], batch=dec_batch, seq=dec_seq)
            mrg = merge(h, ya, yb, yc, w_gate_b[l], b_gate[l], w_a_b[l], w_b_b[l], w_c_b[l])
            x = matmul(mrg, w_out_b[l], tm=1024, tn=1024, epilogue="resid", resid=x, gate=gt1,
                       rows_per_mod=rpm)
            h2 = norm_mod(x, norm2_g[l], sc2, sh2, rows_per_mod=rpm)
            u = matmul(h2, w1_b[l], tm=1024, tn=1024, epilogue="relu2", out_dtype=BF16)
            x = matmul(u, w2_b[l], tm=1024, tn=1024, tk=2048, epilogue="resid", resid=x, gate=gt2,
                       rows_per_mod=rpm)
            new_x.append(x)
        xp, xs = new_x

    y_prompt = norm_mod(xp, final_g, out_dtype=F32).reshape(batch, seq, d)
    y_sample = norm_mod(xs, final_g, out_dtype=F32).reshape(dec_batch, dec_seq, d)

    def stack(parts, heads):
        return jnp.stack([p.reshape(batch, seq, heads, HEAD_DIM) for p in parts], axis=1)

    return (y_prompt, y_sample, stack(wk_list, WIN_KV_HEADS), stack(wv_list, WIN_KV_HEADS),
            stack(nk_list, NA_HEADS), stack(nv_list, NA_HEADS))
```

```python
import functools
import math

import jax
import jax.numpy as jnp
from jax import lax
from jax.experimental import pallas as pl
from jax.experimental.pallas import tpu as pltpu

F32 = jnp.float32
BF16 = jnp.bfloat16

D_MODEL = 4096
DEPTH = 4
GRID_W = 64
HEAD_DIM = 128
FNET_GROUP_W = 256
FNET_W = 1024
WIN_Q_HEADS = 16
WIN_KV_HEADS = 4
WIN_GROUP = WIN_Q_HEADS // WIN_KV_HEADS
WIN = 128
WIN_Q_W = WIN_Q_HEADS * HEAD_DIM
WIN_KV_W = WIN_KV_HEADS * HEAD_DIM
NA_HEADS = 8
NA_KH = 8
NA_KW = 16
NA_W = NA_HEADS * HEAD_DIM
MLP_HIDDEN = 4 * D_MODEL
IN_W = 7168
ROPE_BASE = 10000.0
EPS = 1e-6
NEG_INF = -1e30
ATTN_SCALE = HEAD_DIM ** -0.5
N_MOD = 8

COL_A, COL_BQ, COL_BK, COL_BV, COL_CQ, COL_CK, COL_CV = 0, 1024, 3072, 3584, 4096, 5120, 6144
PROJ_TN = 1024

NA_QROWS = 4
NA_KROWS = NA_QROWS + NA_KH

V7X_SCOPED_VMEM_BYTES = 60000 * 1024
VMEM_MARGIN_BYTES = 6 << 20


def _cparams(semantics, est_bytes):
    limit = min(V7X_SCOPED_VMEM_BYTES, int(est_bytes) + VMEM_MARGIN_BYTES)
    return pltpu.CompilerParams(dimension_semantics=semantics, vmem_limit_bytes=limit)


def _nbytes(shape, dtype):
    return math.prod(shape) * jnp.dtype(dtype).itemsize


def _adaln_kernel(c_ref, w_ref, b_ref, o_ref):
    c = c_ref[...]
    s = (c * (1.0 / (1.0 + jnp.exp(-c)))).astype(BF16)
    o_ref[...] = jnp.dot(s, w_ref[...].astype(BF16), preferred_element_type=F32) + b_ref[...]


def adaln_all(cvec, w_mod, b_mod, tn=512):
    depth, d, n = w_mod.shape
    est = 2 * _nbytes((d, tn), F32) + _nbytes((d, tn), BF16) + 4 * _nbytes((N_MOD, d), F32)
    return pl.pallas_call(
        _adaln_kernel,
        grid=(depth, n // tn),
        in_specs=[
            pl.BlockSpec((N_MOD, d), lambda l, j: (0, 0)),
            pl.BlockSpec((None, d, tn), lambda l, j: (l, 0, j)),
            pl.BlockSpec((None, 1, tn), lambda l, j: (l, 0, j)),
        ],
        out_specs=pl.BlockSpec((None, N_MOD, tn), lambda l, j: (l, 0, j)),
        out_shape=jax.ShapeDtypeStruct((depth, N_MOD, n), F32),
        compiler_params=_cparams(("arbitrary", "arbitrary"), est),
        name="adaln",
    )(cvec, w_mod, b_mod.reshape(depth, 1, n))


def _norm_kernel(*refs, modulated):
    if modulated:
        x_ref, g_ref, sc_ref, sh_ref, o_ref = refs
    else:
        x_ref, g_ref, o_ref = refs
    x = x_ref[...]
    y = x * lax.rsqrt(jnp.mean(x * x, axis=-1, keepdims=True) + EPS)
    y = y * g_ref[...]
    if modulated:
        y = y * (1.0 + sc_ref[...]) + sh_ref[...]
    o_ref[...] = y.astype(o_ref.dtype)


def norm_mod(x, g, sc=None, sh=None, *, rows_per_mod=None, out_dtype=BF16, tr=256):
    m, d = x.shape
    modulated = sc is not None
    in_specs = [pl.BlockSpec((tr, d), lambda i: (i, 0)), pl.BlockSpec((1, d), lambda i: (0, 0))]
    args = [x, g.reshape(1, d)]
    if modulated:
        mod_spec = pl.BlockSpec((None, 1, d), lambda i: ((i * tr) // rows_per_mod, 0, 0))
        in_specs += [mod_spec, mod_spec]
        args += [sc, sh]
    est = 2 * _nbytes((tr, d), F32) + 2 * _nbytes((tr, d), out_dtype) + 3 * _nbytes((tr, d), F32)
    return pl.pallas_call(
        functools.partial(_norm_kernel, modulated=modulated),
        grid=(m // tr,),
        in_specs=in_specs,
        out_specs=pl.BlockSpec((tr, d), lambda i: (i, 0)),
        out_shape=jax.ShapeDtypeStruct((m, d), out_dtype),
        compiler_params=_cparams(("arbitrary",), est),
        name="norm_mod",
    )(*args)


def _epilogue(acc, epilogue, r_ref, gate_ref, o_ref):
    if epilogue == "relu2":
        a = jnp.maximum(acc, 0.0)
        o_ref[...] = (a * a).astype(o_ref.dtype)
    elif epilogue == "resid":
        o_ref[...] = (r_ref[...] + gate_ref[...] * acc).astype(o_ref.dtype)
    else:
        o_ref[...] = acc.astype(o_ref.dtype)


def _mm_kernel(*refs, epilogue, nk):
    if epilogue == "resid":
        x_ref, w_ref, r_ref, gate_ref = refs[:4]
        rest = refs[4:]
    else:
        x_ref, w_ref = refs[:2]
        r_ref = gate_ref = None
        rest = refs[2:]
    o_ref = rest[0]

    def part():
        return jnp.dot(x_ref[...], w_ref[...], preferred_element_type=F32)

    if nk == 1:
        _epilogue(part(), epilogue, r_ref, gate_ref, o_ref)
        return
    acc_ref = rest[1]
    k = pl.program_id(2)

    @pl.when(k == 0)
    def _():
        acc_ref[...] = part()

    @pl.when((k > 0) & (k < nk - 1))
    def _():
        acc_ref[...] += part()

    @pl.when(k == nk - 1)
    def _():
        _epilogue(acc_ref[...] + part(), epilogue, r_ref, gate_ref, o_ref)


def matmul(x, w, *, tm, tn, tk=None, epilogue="plain", out_dtype=F32, resid=None, gate=None,
           rows_per_mod=None):
    m, kdim = x.shape
    n = w.shape[1]
    tk = kdim if tk is None else tk
    nk = kdim // tk
    grid = (m // tm, n // tn, nk)
    in_specs = [pl.BlockSpec((tm, tk), lambda i, j, k: (i, k)),
                pl.BlockSpec((tk, tn), lambda i, j, k: (k, j))]
    args = [x, w]
    est = 2 * _nbytes((tm, tk), x.dtype) + 2 * _nbytes((tk, tn), w.dtype)
    est += 2 * _nbytes((tm, tn), out_dtype) + 2 * _nbytes((tm, tn), F32)
    if epilogue == "resid":
        in_specs += [pl.BlockSpec((tm, tn), lambda i, j, k: (i, j)),
                     pl.BlockSpec((None, 1, tn), lambda i, j, k: ((i * tm) // rows_per_mod, 0, j))]
        args += [resid, gate]
        est += 2 * _nbytes((tm, tn), F32)
    scratch = [pltpu.VMEM((tm, tn), F32)] if nk > 1 else []
    return pl.pallas_call(
        functools.partial(_mm_kernel, epilogue=epilogue, nk=nk),
        grid=grid,
        in_specs=in_specs,
        out_specs=pl.BlockSpec((tm, tn), lambda i, j, k: (i, j)),
        out_shape=jax.ShapeDtypeStruct((m, n), out_dtype),
        scratch_shapes=scratch,
        compiler_params=_cparams(("arbitrary", "arbitrary", "arbitrary"), est),
        name="mm_" + epilogue,
    )(*args)


def _rope(x, cos, sin_lo, sin_hi):
    quarter = HEAD_DIM // 4
    return (x * cos + pltpu.roll(x, HEAD_DIM - quarter, 1) * sin_lo + pltpu.roll(x, quarter, 1) * sin_hi)


def _proj_tile_classes(rope):
    heads_per_tile = PROJ_TN // HEAD_DIM
    classes = {}
    for j in range(IN_W // PROJ_TN):
        ops = []
        for hh in range(heads_per_tile):
            col = j * PROJ_TN + hh * HEAD_DIM
            is_bq = COL_BQ <= col < COL_BK
            is_bk = COL_BK <= col < COL_BV
            is_cq = COL_CQ <= col < COL_CK
            ops.append((rope and (is_bq or is_bk), is_bq or is_cq))
        classes.setdefault(tuple(ops), []).append(j)
    return classes


def _proj_kernel(*refs, rope):
    if rope:
        h_ref, w_ref, cos_ref, slo_ref, shi_ref, o32_ref, o16_ref = refs
    else:
        h_ref, w_ref, o32_ref, o16_ref = refs
    j = pl.program_id(1)

    def run(ops):
        acc = jnp.dot(h_ref[...], w_ref[...], preferred_element_type=F32)
        o32_ref[...] = acc
        for hh, (do_rope, do_scale) in enumerate(ops):
            cols = slice(hh * HEAD_DIM, (hh + 1) * HEAD_DIM)
            x = acc[:, cols]
            if do_rope:
                x = _rope(x, cos_ref[...], slo_ref[...], shi_ref[...])
            if do_scale:
                x = x * ATTN_SCALE
            o16_ref[:, cols] = x.astype(BF16)

    for ops, tiles in _proj_tile_classes(rope).items():
        cond = j == tiles[0]
        for t in tiles[1:]:
            cond = cond | (j == t)
        pl.when(cond)(functools.partial(run, ops))


def in_projection(h, w, rope_tabs=None, *, seq=None, tm=1024):
    m, d = h.shape
    n = w.shape[1]
    tn = PROJ_TN
    rope = rope_tabs is not None
    in_specs = [pl.BlockSpec((tm, d), lambda i, j: (i, 0)), pl.BlockSpec((d, tn), lambda i, j: (0, j))]
    args = [h, w]
    if rope:
        tiles_per_seq = seq // tm
        tab_spec = pl.BlockSpec((tm, HEAD_DIM), lambda i, j: (i % tiles_per_seq, 0))
        in_specs += [tab_spec] * 3
        args += list(rope_tabs)
    out_spec = pl.BlockSpec((tm, tn), lambda i, j: (i, j))
    est = 2 * _nbytes((tm, d), BF16) + 2 * _nbytes((d, tn), BF16) + 2 * _nbytes((tm, tn), F32)
    est += 2 * _nbytes((tm, tn), BF16) + 2 * _nbytes((tm, tn), F32) + 6 * _nbytes((tm, HEAD_DIM), F32)
    return pl.pallas_call(
        functools.partial(_proj_kernel, rope=rope),
        grid=(m // tm, n // tn),
        in_specs=in_specs,
        out_specs=[out_spec, out_spec],
        out_shape=[jax.ShapeDtypeStruct((m, n), F32), jax.ShapeDtypeStruct((m, n), BF16)],
        compiler_params=_cparams(("arbitrary", "arbitrary"), est),
        name="in_proj",
    )(*args)


def _sigmoid(z):
    return 1.0 / (1.0 + jnp.exp(-z))


def _merge_kernel(h_ref, ya_ref, yb_ref, yc_ref, wga_ref, wgb_ref, wgc_ref, bga_ref, bgb_ref, bgc_ref,
                  wa_ref, wb_ref, wc_ref, o_ref):
    h = h_ref[...]

    def branch(wg_ref, bg_ref, y_ref, w_ref):
        gate = _sigmoid(jnp.dot(h, wg_ref[...], preferred_element_type=F32) + bg_ref[...])
        return gate * jnp.dot(y_ref[...], w_ref[...], preferred_element_type=F32)

    m = branch(wga_ref, bga_ref, ya_ref, wa_ref)
    m = m + branch(wgb_ref, bgb_ref, yb_ref, wb_ref)
    m = m + branch(wgc_ref, bgc_ref, yc_ref, wc_ref)
    o_ref[...] = m.astype(o_ref.dtype)


def merge(h, ya, yb, yc, w_gate, b_gate, w_a, w_b, w_c, *, tm=512, tn=256):
    m, d = h.shape
    nj = d // tn
    b_gate = b_gate.reshape(1, 3 * d)
    row = lambda i, j: (i, 0)
    in_specs = [
        pl.BlockSpec((tm, d), row),
        pl.BlockSpec((tm, ya.shape[1]), row),
        pl.BlockSpec((tm, yb.shape[1]), row),
        pl.BlockSpec((tm, yc.shape[1]), row),
        pl.BlockSpec((d, tn), lambda i, j: (0, j)),
        pl.BlockSpec((d, tn), lambda i, j: (0, nj + j)),
        pl.BlockSpec((d, tn), lambda i, j: (0, 2 * nj + j)),
        pl.BlockSpec((1, tn), lambda i, j: (0, j)),
        pl.BlockSpec((1, tn), lambda i, j: (0, nj + j)),
        pl.BlockSpec((1, tn), lambda i, j: (0, 2 * nj + j)),
        pl.BlockSpec((w_a.shape[0], tn), lambda i, j: (0, j)),
        pl.BlockSpec((w_b.shape[0], tn), lambda i, j: (0, j)),
        pl.BlockSpec((w_c.shape[0], tn), lambda i, j: (0, j)),
    ]
    est = 2 * 2 * _nbytes((tm, d), BF16) + 2 * 4 * _nbytes((d, tn), BF16) + 8 * _nbytes((tm, tn), F32)
    return pl.pallas_call(
        _merge_kernel,
        grid=(m // tm, nj),
        in_specs=in_specs,
        out_specs=pl.BlockSpec((tm, tn), lambda i, j: (i, j)),
        out_shape=jax.ShapeDtypeStruct((m, d), BF16),
        compiler_params=_cparams(("arbitrary", "arbitrary"), est),
        name="merge",
    )(h, ya, yb, yc, w_gate, w_gate, w_gate, b_gate, b_gate, b_gate, w_a, w_b, w_c)


def _split_hi_lo(x):
    hi = x.astype(BF16)
    lo = (x - hi.astype(F32)).astype(BF16)
    return hi, lo


def _dot3(xh, xl, wh, wl):
    acc = jnp.dot(xh, wh, preferred_element_type=F32)
    acc += jnp.dot(xl, wh, preferred_element_type=F32)
    acc += jnp.dot(xh, wl, preferred_element_type=F32)
    return acc


def _dot_nt(a, b):
    return lax.dot_general(a, b, (((1,), (1,)), ((), ())), preferred_element_type=F32)


def _row_max(scores, extra=None):
    m = scores[0].max(axis=-1, keepdims=True)
    for s in scores[1:]:
        m = jnp.maximum(m, s.max(axis=-1, keepdims=True))
    if extra is not None:
        m = jnp.maximum(m, extra)
    return m


def _sink_column(sink_ref, hk, rows_per_head):
    rows = WIN_GROUP * rows_per_head
    group = lax.broadcasted_iota(jnp.int32, (rows, 1), 0) // rows_per_head
    sink = jnp.zeros((rows, 1), F32)
    for g in range(WIN_GROUP):
        sink = jnp.where(group == g, sink_ref[hk * WIN_GROUP + g], sink)
    return sink


def _stacked_q(q_refs, hk):
    heads_per_ref = q_refs[0].shape[1] // HEAD_DIM
    parts = []
    for g in range(WIN_GROUP):
        h = hk * WIN_GROUP + g
        ref = q_refs[h // heads_per_ref]
        lo = (h % heads_per_ref) * HEAD_DIM
        parts.append(ref[:, lo:lo + HEAD_DIM])
    return jnp.concatenate(parts, axis=0)


def _ctx_mixer_kernel(a_ref, bq0_ref, bq1_ref, bk_ref, bv_ref, cq_ref, ck_ref, cv_ref, sink_ref,
                      ch_h_ref, ch_l_ref, pc_h_ref, pc_l_ref, ps_h_ref, ps_l_ref,
                      ya_ref, yb_ref, yc_ref, *, seq):
    gw = FNET_GROUP_W
    ch_h = ch_h_ref[...]
    ch_l = ch_l_ref[...]
    pc_h, pc_l, ps_h, ps_l = pc_h_ref[...], pc_l_ref[...], ps_h_ref[...], ps_l_ref[...]
    norm = 1.0 / math.sqrt(seq * gw)
    for g in range(FNET_W // gw):
        a_h, a_l = _split_hi_lo(a_ref[:, g * gw:(g + 1) * gw])
        t = _dot3(a_h, a_l, ch_h, ch_l)
        tc_h, tc_l = _split_hi_lo(t[:, :gw])
        ts_h, ts_l = _split_hi_lo(t[:, gw:])
        y = _dot3(pc_h, pc_l, tc_h, tc_l) - _dot3(ps_h, ps_l, ts_h, ts_l)
        ya_ref[:, g * gw:(g + 1) * gw] = (y * norm).astype(ya_ref.dtype)

    hd = HEAD_DIM
    for hk in range(WIN_KV_HEADS):
        q = _stacked_q((bq0_ref, bq1_ref), hk)
        s = _dot_nt(q, bk_ref[:, hk * hd:(hk + 1) * hd])
        sink = _sink_column(sink_ref, hk, seq)
        m = _row_max([s], sink)
        e = jnp.exp(s - m)
        den = e.sum(axis=-1, keepdims=True) + jnp.exp(sink - m)
        o = jnp.dot(e.astype(BF16), bv_ref[:, hk * hd:(hk + 1) * hd], preferred_element_type=F32) / den
        for g in range(WIN_GROUP):
            h = hk * WIN_GROUP + g
            yb_ref[:, h * hd:(h + 1) * hd] = o[g * seq:(g + 1) * seq].astype(yb_ref.dtype)

    for h in range(NA_HEADS):
        hs = slice(h * hd, (h + 1) * hd)
        s = _dot_nt(cq_ref[:, hs], ck_ref[:, hs])
        m = s.max(axis=-1, keepdims=True)
        e = jnp.exp(s - m)
        den = e.sum(axis=-1, keepdims=True)
        o = jnp.dot(e.astype(BF16), cv_ref[:, hs], preferred_element_type=F32) / den
        yc_ref[:, hs] = o.astype(yc_ref.dtype)


def ctx_mixers(p32, p16, sink, tabs, *, batch, seq):
    def col(width, offset):
        return pl.BlockSpec((seq, width), lambda b: (b, offset // width))

    const = lambda shape: pl.BlockSpec(shape, lambda b: (0,) * len(shape))
    half_q = WIN_Q_W // 2
    in_specs = [
        col(FNET_W, COL_A), col(half_q, COL_BQ), col(half_q, COL_BQ + half_q),
        col(WIN_KV_W, COL_BK), col(WIN_KV_W, COL_BV),
        col(NA_W, COL_CQ), col(NA_W, COL_CK), col(NA_W, COL_CV),
        pl.BlockSpec(memory_space=pltpu.SMEM),
        const(tabs["ch_h"].shape), const(tabs["ch_l"].shape),
        const(tabs["pc_h"].shape), const(tabs["pc_l"].shape),
        const(tabs["ps_h"].shape), const(tabs["ps_l"].shape),
    ]
    out_specs = [pl.BlockSpec((seq, FNET_W), lambda b: (b, 0)),
                 pl.BlockSpec((seq, WIN_Q_W), lambda b: (b, 0)),
                 pl.BlockSpec((seq, NA_W), lambda b: (b, 0))]
    m = batch * seq
    out_shape = [jax.ShapeDtypeStruct((m, FNET_W), BF16), jax.ShapeDtypeStruct((m, WIN_Q_W), BF16),
                 jax.ShapeDtypeStruct((m, NA_W), BF16)]
    est = 2 * _nbytes((seq, IN_W), F32) + 2 * _nbytes((seq, D_MODEL), BF16) + (16 << 20)
    return pl.pallas_call(
        functools.partial(_ctx_mixer_kernel, seq=seq),
        grid=(batch,),
        in_specs=in_specs,
        out_specs=out_specs,
        out_shape=out_shape,
        compiler_params=_cparams(("arbitrary",), est),
        name="ctx_mixers",
    )(p32, p16, p16, p16, p16, p16, p16, p16, sink,
      tabs["ch_h"], tabs["ch_l"], tabs["pc_h"], tabs["pc_l"], tabs["ps_h"], tabs["ps_l"])


def _group_dot3(x, tab_h, tab_l):
    gw = FNET_GROUP_W
    outs = []
    for g in range(FNET_W // gw):
        x_h, x_l = _split_hi_lo(x[:, g * gw:(g + 1) * gw])
        outs.append(_dot3(x_h, x_l, tab_h, tab_l))
    return jnp.concatenate(outs, axis=1)


def _dft_fold_kernel(a_ref, arev_ref, mid_ref, ch_h_ref, ch_l_ref, te_ref, to_ref, aux_ref, alt_ref, *, nt):
    i = pl.program_id(1)
    gw = FNET_GROUP_W
    cc_h, cc_l = ch_h_ref[:, :gw], ch_l_ref[:, :gw]
    sc_h, sc_l = ch_h_ref[:, gw:], ch_l_ref[:, gw:]
    a = a_ref[...]
    arev = arev_ref[...]
    tm = a.shape[0]
    row = lax.broadcasted_iota(jnp.int32, (tm, 1), 0)
    e = a + arev
    o = jnp.where((row == 0) & (i == 0), 0.0, a - arev)

    te_h, te_l = _split_hi_lo(_group_dot3(e, cc_h, cc_l))
    to_h, to_l = _split_hi_lo(_group_dot3(o, sc_h, sc_l))
    te_ref[0], te_ref[1], te_ref[2] = te_h, te_h, te_l
    to_ref[0], to_ref[1], to_ref[2] = to_h, to_h, to_l

    sign = (1 - 2 * (row % 2)).astype(F32)
    alt = jnp.sum(e * sign, axis=0, keepdims=True)

    @pl.when(i == 0)
    def _():
        alt_ref[...] = jnp.zeros_like(alt_ref)

    alt_ref[0:1, :] += alt

    @pl.when(i == nt - 1)
    def _():
        mid = mid_ref[0:1, :]
        rows8 = lax.broadcasted_iota(jnp.int32, alt_ref.shape, 0)
        x = jnp.where(rows8 == 0, mid, jnp.where(rows8 == 1, alt_ref[0:1, :] + mid, 0.0))
        aux_ref[...] = _group_dot3(x, cc_h, cc_l)


def dft_fold(p32, arev, tabs, *, batch, seq, tm=512):
    half = seq // 2
    nt = half // tm
    sub = 8
    w = FNET_W
    slab = pl.BlockSpec((3, tm, w), lambda b, i: (0, i, b))
    est = 4 * _nbytes((tm, w), F32) + 4 * 3 * _nbytes((tm, w), BF16) + (12 << 20)
    return pl.pallas_call(
        functools.partial(_dft_fold_kernel, nt=nt),
        grid=(batch, nt),
        in_specs=[pl.BlockSpec((tm, w), lambda b, i: (b * (seq // tm) + i, COL_A // w)),
                  pl.BlockSpec((tm, w), lambda b, i: (b * nt + i, 0)),
                  pl.BlockSpec((sub, w), lambda b, i: ((b * seq + half) // sub, COL_A // w)),
                  pl.BlockSpec(tabs["ch_h"].shape, lambda b, i: (0, 0)),
                  pl.BlockSpec(tabs["ch_l"].shape, lambda b, i: (0, 0))],
        out_specs=[slab, slab, pl.BlockSpec((None, sub, w), lambda b, i: (b, 0, 0))],
        out_shape=[jax.ShapeDtypeStruct((3, half, batch * w), BF16),
                   jax.ShapeDtypeStruct((3, half, batch * w), BF16),
                   jax.ShapeDtypeStruct((batch, sub, w), F32)],
        scratch_shapes=[pltpu.VMEM((sub, w), F32)],
        compiler_params=_cparams(("arbitrary", "arbitrary"), est),
        name="dft_fold",
    )(p32, arev, p32, tabs["ch_h"], tabs["ch_l"])


def _dft_pos_kernel(c3_ref, s3_ref, te_ref, to_ref, aux_ref, lo_ref, hi_ref, accc_ref, accs_ref, *, nk, norm):
    i = pl.program_id(0)
    k = pl.program_id(2)

    def dots():
        return (jnp.dot(c3_ref[...], te_ref[...], preferred_element_type=F32),
                jnp.dot(s3_ref[...], to_ref[...], preferred_element_type=F32))

    @pl.when(k == 0)
    def _():
        accc_ref[...], accs_ref[...] = dots()

    @pl.when((k > 0) & (k < nk - 1))
    def _():
        dc, ds = dots()
        accc_ref[...] += dc
        accs_ref[...] += ds

    @pl.when(k == nk - 1)
    def _():
        dc, ds = dots()
        tm = dc.shape[0]
        row = lax.broadcasted_iota(jnp.int32, (tm, 1), 0)
        mid_c = aux_ref[0:1, :]
        pc = accc_ref[...] + dc + jnp.where(row % 2 == 0, mid_c, -mid_c)
        ps = accs_ref[...] + ds
        lo_ref[...] = ((pc - ps) * norm).astype(lo_ref.dtype)
        hi = jnp.where((row == 0) & (i == 0), aux_ref[1:2, :], pc + ps)
        hi_ref[...] = (hi * norm).astype(hi_ref.dtype)


def dft_pos(te, to, aux, tabs, *, batch, seq, tm=1024, tk=1024):
    half = seq // 2
    w = FNET_W
    kdim = 3 * half
    nk = kdim // tk
    ni = half // tm
    norm = 1.0 / math.sqrt(seq * FNET_GROUP_W)
    tab_spec = pl.BlockSpec((tm, tk), lambda i, j, k: (i, k))
    slab_spec = pl.BlockSpec((tk, w), lambda i, j, k: (k, j))
    out_spec = pl.BlockSpec((tm, w), lambda i, j, k: (j * ni + i, 0))
    est = 4 * _nbytes((tm, tk), BF16) + 4 * _nbytes((tk, w), BF16) + 4 * _nbytes((tm, w), F32)
    est += 4 * _nbytes((tm, w), BF16) + 2 * _nbytes((tm, w), F32)
    return pl.pallas_call(
        functools.partial(_dft_pos_kernel, nk=nk, norm=norm),
        grid=(ni, batch, nk),
        in_specs=[tab_spec, tab_spec, slab_spec, slab_spec,
                  pl.BlockSpec((None,) + aux.shape[1:], lambda i, j, k: (j, 0, 0))],
        out_specs=[out_spec, out_spec],
        out_shape=[jax.ShapeDtypeStruct((batch * half, w), BF16)] * 2,
        scratch_shapes=[pltpu.VMEM((tm, w), F32), pltpu.VMEM((tm, w), F32)],
        compiler_params=_cparams(("arbitrary", "arbitrary", "arbitrary"), est),
        name="dft_pos",
    )(tabs["c3"], tabs["s3"], te.reshape(kdim, batch * w), to.reshape(kdim, batch * w), aux)


def latent_fourier(p32, tabs, *, batch, seq):
    half = seq // 2
    w = FNET_W
    a3 = p32[:, COL_A:COL_A + w].reshape(batch, seq, w)
    arev = jnp.concatenate([jnp.zeros((batch, 1, w), F32), jnp.flip(a3[:, half + 1:], axis=1)], axis=1)
    te, to, aux = dft_fold(p32, arev.reshape(batch * half, w), tabs, batch=batch, seq=seq)
    lo, hi = dft_pos(te, to, aux, tabs, batch=batch, seq=seq)
    hi = jnp.roll(jnp.flip(hi.reshape(batch, half, w), axis=1), 1, axis=1)
    return jnp.concatenate([lo.reshape(batch, half, w), hi], axis=1).reshape(batch * seq, w)


def _win_attn_kernel(q0_ref, q1_ref, k_ref, v_ref, ck_ref, cv_ref, sink_ref, o_ref, *, seq):
    n = pl.program_id(1)
    blk = WIN
    span = 3 * blk
    hd = HEAD_DIM
    start = pl.multiple_of(jnp.clip((n - 1) * blk, 0, seq - span), blk)
    rows = WIN_GROUP * blk
    ipos = n * blk + lax.broadcasted_iota(jnp.int32, (rows, span), 0) % blk
    jpos = start + lax.broadcasted_iota(jnp.int32, (rows, span), 1)
    valid = jnp.abs(ipos - jpos) <= WIN

    for hk in range(WIN_KV_HEADS):
        hs = slice(hk * hd, (hk + 1) * hd)
        q = _stacked_q((q0_ref, q1_ref), hk)
        s = jnp.where(valid, _dot_nt(q, k_ref[pl.ds(start, span), hs]), NEG_INF)
        sc = _dot_nt(q, ck_ref[:, hs].astype(BF16))
        sink = _sink_column(sink_ref, hk, blk)
        m = _row_max([s, sc], sink)
        e = jnp.exp(s - m)
        ec = jnp.exp(sc - m)
        den = e.sum(axis=-1, keepdims=True) + ec.sum(axis=-1, keepdims=True) + jnp.exp(sink - m)
        o = jnp.dot(e.astype(BF16), v_ref[pl.ds(start, span), hs], preferred_element_type=F32)
        o += jnp.dot(ec.astype(BF16), cv_ref[:, hs].astype(BF16), preferred_element_type=F32)
        o = o / den
        for g in range(WIN_GROUP):
            h = hk * WIN_GROUP + g
            o_ref[:, h * hd:(h + 1) * hd] = o[g * blk:(g + 1) * blk].astype(o_ref.dtype)


def win_attention(p16, cache_k, cache_v, sink, *, batch, seq):
    blk = WIN
    nblk = seq // blk
    past = cache_k.shape[1]
    half_q = WIN_Q_W // 2

    def q_spec(offset):
        return pl.BlockSpec((blk, half_q), lambda b, n: (b * nblk + n, offset // half_q))

    def seq_spec(offset):
        return pl.BlockSpec((seq, WIN_KV_W), lambda b, n: (b, offset // WIN_KV_W))

    cache_spec = pl.BlockSpec((None, past, WIN_KV_W), lambda b, n: (b, 0, 0))
    in_specs = [q_spec(COL_BQ), q_spec(COL_BQ + half_q), seq_spec(COL_BK), seq_spec(COL_BV),
                cache_spec, cache_spec, pl.BlockSpec(memory_space=pltpu.SMEM)]
    est = 4 * _nbytes((seq, WIN_KV_W), BF16) + 4 * _nbytes((past, WIN_KV_W), F32) + (16 << 20)
    return pl.pallas_call(
        functools.partial(_win_attn_kernel, seq=seq),
        grid=(batch, nblk),
        in_specs=in_specs,
        out_specs=pl.BlockSpec((blk, WIN_Q_W), lambda b, n: (b * nblk + n, 0)),
        out_shape=jax.ShapeDtypeStruct((batch * seq, WIN_Q_W), BF16),
        compiler_params=_cparams(("arbitrary", "arbitrary"), est),
        name="win_attn",
    )(p16, p16, p16, p16, cache_k, cache_v, sink)


def _na_window_start(rg, rows_n):
    return jnp.clip(rg * NA_QROWS - NA_KH // 2, 0, rows_n - NA_KROWS)


def _na_attn_kernel(q_ref, k_ref, v_ref, ck_ref, cv_ref, bias_ref, o_ref, *, rows_n):
    rg = pl.program_id(1)
    hd = HEAD_DIM
    nkeys = NA_KROWS * GRID_W
    start = pl.multiple_of(_na_window_start(rg, rows_n) * GRID_W, GRID_W)
    for h in range(NA_HEADS):
        hs = slice(h * hd, (h + 1) * hd)
        q = q_ref[:, hs]
        s = _dot_nt(q, k_ref[pl.ds(start, nkeys), hs]) + bias_ref[h]
        sc = _dot_nt(q, ck_ref[:, hs].astype(BF16))
        m = _row_max([s, sc])
        e = jnp.exp(s - m)
        ec = jnp.exp(sc - m)
        den = e.sum(axis=-1, keepdims=True) + ec.sum(axis=-1, keepdims=True)
        o = jnp.dot(e.astype(BF16), v_ref[pl.ds(start, nkeys), hs], preferred_element_type=F32)
        o += jnp.dot(ec.astype(BF16), cv_ref[:, hs].astype(BF16), preferred_element_type=F32)
        o_ref[:, hs] = (o / den).astype(o_ref.dtype)


def na_attention(p16, cache_k, cache_v, bias_tab, *, batch, seq):
    rows_n = seq // GRID_W
    ngroups = rows_n // NA_QROWS
    nq = NA_QROWS * GRID_W
    nkeys = NA_KROWS * GRID_W
    past = cache_k.shape[1]

    def seq_spec(offset):
        return pl.BlockSpec((seq, NA_W), lambda b, rg: (b, offset // NA_W), pipeline_mode=pl.Buffered(1))

    def pattern(b, rg):
        return ((rg > 0).astype(jnp.int32) + (rg == ngroups - 1).astype(jnp.int32), 0, 0, 0)

    cache_spec = pl.BlockSpec((None, past, NA_W), lambda b, rg: (b, 0, 0))
    in_specs = [pl.BlockSpec((nq, NA_W), lambda b, rg: (b * ngroups + rg, COL_CQ // NA_W)),
                seq_spec(COL_CK), seq_spec(COL_CV), cache_spec, cache_spec,
                pl.BlockSpec((None, NA_HEADS, nq, nkeys), pattern, pipeline_mode=pl.Buffered(1))]
    est = 2 * _nbytes((seq, NA_W), BF16) + _nbytes((NA_HEADS, nq, nkeys), F32)
    est += 4 * _nbytes((past, NA_W), F32) + (12 << 20)
    return pl.pallas_call(
        functools.partial(_na_attn_kernel, rows_n=rows_n),
        grid=(batch, ngroups),
        in_specs=in_specs,
        out_specs=pl.BlockSpec((nq, NA_W), lambda b, rg: (b * ngroups + rg, 0)),
        out_shape=jax.ShapeDtypeStruct((batch * seq, NA_W), BF16),
        compiler_params=_cparams(("arbitrary", "arbitrary"), est),
        name="na_attn",
    )(p16, p16, p16, cache_k, cache_v, bias_tab)


def _dft_cos_sin(n):
    idx = jnp.arange(n, dtype=jnp.int32)
    phase = (idx[:, None] * idx[None, :]) % n
    ang = phase.astype(F32) * (2.0 * math.pi / n)
    return jnp.cos(ang), jnp.sin(ang)


def _hi_lo(x):
    hi = x.astype(BF16)
    return hi, (x - hi.astype(F32)).astype(BF16)


def _fourier_tables(ctx_seq, lat_seq):
    cc, sc = _dft_cos_sin(FNET_GROUP_W)
    ch_h, ch_l = _hi_lo(jnp.concatenate([cc, sc], axis=1))
    cp, sp = _dft_cos_sin(ctx_seq)
    pc_h, pc_l = _hi_lo(cp)
    ps_h, ps_l = _hi_lo(sp)
    half = lat_seq // 2
    idx = jnp.arange(half, dtype=jnp.int32)
    ang = ((idx[:, None] * idx[None, :]) % lat_seq).astype(F32) * (2.0 * math.pi / lat_seq)
    c_h, c_l = _hi_lo(jnp.cos(ang))
    s_h, s_l = _hi_lo(jnp.sin(ang))
    c3 = jnp.concatenate([c_h, c_l, c_h], axis=1)
    s3 = jnp.concatenate([s_h, s_l, s_h], axis=1)
    return dict(ch_h=ch_h, ch_l=ch_l, pc_h=pc_h, pc_l=pc_l, ps_h=ps_h, ps_l=ps_l, c3=c3, s3=s3)


def _rope_tables(seq):
    t = jnp.arange(seq)
    n_freq = HEAD_DIM // 4
    inv = ROPE_BASE ** (-jnp.arange(n_freq, dtype=F32) / n_freq)
    ang_r = (t // GRID_W).astype(F32)[:, None] * inv
    ang_c = (t % GRID_W).astype(F32)[:, None] * inv
    zeros = jnp.zeros_like(ang_r)
    cos = jnp.concatenate([jnp.cos(ang_r)] * 2 + [jnp.cos(ang_c)] * 2, axis=1)
    sin_lo = jnp.concatenate([-jnp.sin(ang_r), zeros, -jnp.sin(ang_c), zeros], axis=1)
    sin_hi = jnp.concatenate([zeros, jnp.sin(ang_r), zeros, jnp.sin(ang_c)], axis=1)
    return cos, sin_lo, sin_hi


def _na_bias_tables(rpb, rows_n):
    depth = rpb.shape[0]
    col = jnp.arange(GRID_W)
    cs = jnp.clip(col - NA_KW // 2, 0, GRID_W - NA_KW)
    col_ok = (col[None, :] >= cs[:, None]) & (col[None, :] < cs[:, None] + NA_KW)
    idx_c = jnp.clip(col[None, :] - col[:, None], -(NA_KW - 1), NA_KW - 1) + NA_KW - 1
    bt = jnp.where(col_ok[None, None, None], rpb.astype(F32)[:, :, :, idx_c], NEG_INF)

    ngroups = rows_n // NA_QROWS
    groups = jnp.array([0, 1, ngroups - 1])
    r = groups[:, None] * NA_QROWS + jnp.arange(NA_QROWS)[None, :]
    ws = jnp.clip(groups * NA_QROWS - NA_KH // 2, 0, rows_n - NA_KROWS)
    krow = ws[:, None] + jnp.arange(NA_KROWS)[None, :]
    rs = jnp.clip(r - NA_KH // 2, 0, rows_n - NA_KH)
    row_ok = (krow[:, None, :] >= rs[:, :, None]) & (krow[:, None, :] < rs[:, :, None] + NA_KH)
    rho = jnp.clip(krow[:, None, :] - r[:, :, None] + NA_KH - 1, 0, 2 * NA_KH - 2)
    tab = jnp.where(row_ok[None, None, :, :, :, None, None], bt[:, :, rho], NEG_INF)
    tab = jnp.transpose(tab, (0, 2, 1, 3, 5, 4, 6))
    return tab.reshape(depth, 3, NA_HEADS, NA_QROWS * GRID_W, NA_KROWS * GRID_W)


def kernel(x_prompt, x_sample, cache_win_k, cache_win_v, cache_na_k, cache_na_v, c, c_ctx, w_mod, b_mod,
           norm1_g, w_in, win_sink, na_rpb, w_a, w_b, w_c, w_gate, b_gate, w_out, norm2_g, w1, w2, final_g):
    batch, seq, d = x_prompt.shape
    dec_batch, dec_seq, _ = x_sample.shape
    past = cache_win_k.shape[2]
    m_ctx, m_lat = batch * seq, dec_batch * dec_seq

    w_in_b = w_in.astype(BF16)
    w_a_b, w_b_b, w_c_b = w_a.astype(BF16), w_b.astype(BF16), w_c.astype(BF16)
    w_gate_b, w_out_b = w_gate.astype(BF16), w_out.astype(BF16)
    w1_b, w2_b = w1.astype(BF16), w2.astype(BF16)

    cvec = jnp.concatenate([c_ctx[None], c, jnp.zeros((N_MOD - 1 - dec_batch, d), F32)], axis=0)
    mods = adaln_all(cvec, w_mod, b_mod).reshape(DEPTH, N_MOD, 6, 1, d)

    ftabs = _fourier_tables(seq, dec_seq)
    rope_tabs = _rope_tables(dec_seq)
    na_bias = _na_bias_tables(na_rpb, dec_seq // GRID_W)

    xp = x_prompt.reshape(m_ctx, d)
    xs = x_sample.reshape(m_lat, d)
    groups = (
        dict(rows_per_mod=m_ctx, mod=slice(0, 1)),
        dict(rows_per_mod=dec_seq, mod=slice(1, 1 + dec_batch)),
    )
    wk_list, wv_list, nk_list, nv_list = [], [], [], []
    for l in range(DEPTH):
        new_x = []
        for gi, (x, grp) in enumerate(zip((xp, xs), groups)):
            rpm = grp["rows_per_mod"]
            sh1, sc1, gt1, sh2, sc2, gt2 = [mods[l, grp["mod"], k] for k in range(6)]
            h = norm_mod(x, norm1_g[l], sc1, sh1, rows_per_mod=rpm)
            if gi == 0:
                p32, p16 = in_projection(h, w_in_b[l])
                ya, yb, yc = ctx_mixers(p32, p16, win_sink[l], ftabs, batch=batch, seq=seq)
                wk_list.append(p32[:, COL_BK:COL_BK + WIN_KV_W])
                wv_list.append(p32[:, COL_BV:COL_BV + WIN_KV_W])
                nk_list.append(p32[:, COL_CK:COL_CK + NA_W])
                nv_list.append(p32[:, COL_CV:COL_CV + NA_W])
            else:
                p32, p16 = in_projection(h, w_in_b[l], rope_tabs, seq=dec_seq)
                ya = latent_fourier(p32, ftabs, batch=dec_batch, seq=dec_seq)
                yb = win_attention(p16, cache_win_k[:, l].reshape(dec_batch, past, WIN_KV_W),
                                   cache_win_v[:, l].reshape(dec_batch, past, WIN_KV_W),
                                   win_sink[l], batch=dec_batch, seq=dec_seq)
                yc = na_attention(p16, cache_na_k[:, l].reshape(dec_batch, past, NA_W),
                                  cache_na_v[:, l].reshape(dec_batch, past, NA_W),
                                  na_bias[l], batch=dec_batch, seq=dec_seq)
            mrg = merge(h, ya, yb, yc, w_gate_b[l], b_gate[l], w_a_b[l], w_b_b[l], w_c_b[l])
            x = matmul(mrg, w_out_b[l], tm=1024, tn=1024, epilogue="resid", resid=x, gate=gt1,
                       rows_per_mod=rpm)
            h2 = norm_mod(x, norm2_g[l], sc2, sh2, rows_per_mod=rpm)
            u = matmul(h2, w1_b[l], tm=1024, tn=1024, epilogue="relu2", out_dtype=BF16)
            x = matmul(u, w2_b[l], tm=1024, tn=1024, tk=2048, epilogue="resid", resid=x, gate=gt2,
                       rows_per_mod=rpm)
            new_x.append(x)
        xp, xs = new_x

    y_prompt = norm_mod(xp, final_g, out_dtype=F32).reshape(batch, seq, d)
    y_sample = norm_mod(xs, final_g, out_dtype=F32).reshape(dec_batch, dec_seq, d)

    def stack(parts, heads):
        return jnp.stack([p.reshape(batch, seq, heads, HEAD_DIM) for p in parts], axis=1)

    return (y_prompt, y_sample, stack(wk_list, WIN_KV_HEADS), stack(wv_list, WIN_KV_HEADS),
            stack(nk_list, NA_HEADS), stack(nv_list, NA_HEADS))
```

```python
import functools
import math

import jax
import jax.numpy as jnp
from jax import lax
from jax.experimental import pallas as pl
from jax.experimental.pallas import tpu as pltpu

F32 = jnp.float32
BF16 = jnp.bfloat16

D_MODEL = 4096
DEPTH = 4
GRID_W = 64
HEAD_DIM = 128
FNET_GROUP_W = 256
FNET_W = 1024
WIN_Q_HEADS = 16
WIN_KV_HEADS = 4
WIN_GROUP = WIN_Q_HEADS // WIN_KV_HEADS
WIN = 128
WIN_Q_W = WIN_Q_HEADS * HEAD_DIM
WIN_KV_W = WIN_KV_HEADS * HEAD_DIM
NA_HEADS = 8
NA_KH = 8
NA_KW = 16
NA_W = NA_HEADS * HEAD_DIM
MLP_HIDDEN = 4 * D_MODEL
IN_W = 7168
ROPE_BASE = 10000.0
EPS = 1e-6
NEG_INF = -1e30
ATTN_SCALE = HEAD_DIM ** -0.5
N_MOD = 8

COL_A, COL_BQ, COL_BK, COL_BV, COL_CQ, COL_CK, COL_CV = 0, 1024, 3072, 3584, 4096, 5120, 6144
PROJ_TN = 1024

NA_QROWS = 4
NA_KROWS = NA_QROWS + NA_KH

DFT_FOLD_TM = 512
DFT_POS_TM = 1024

V7X_SCOPED_VMEM_BYTES = 60000 * 1024
VMEM_MARGIN_BYTES = 6 << 20


def _cparams(semantics, est_bytes):
    limit = min(V7X_SCOPED_VMEM_BYTES, int(est_bytes) + VMEM_MARGIN_BYTES)
    return pltpu.CompilerParams(dimension_semantics=semantics, vmem_limit_bytes=limit)


def _nbytes(shape, dtype):
    return math.prod(shape) * jnp.dtype(dtype).itemsize


def _adaln_kernel(c_ref, w_ref, b_ref, o_ref):
    c = c_ref[...]
    s = (c * (1.0 / (1.0 + jnp.exp(-c)))).astype(BF16)
    o_ref[...] = jnp.dot(s, w_ref[...].astype(BF16), preferred_element_type=F32) + b_ref[...]


def adaln_all(cvec, w_mod, b_mod, tn=512):
    depth, d, n = w_mod.shape
    est = 2 * _nbytes((d, tn), F32) + _nbytes((d, tn), BF16) + 4 * _nbytes((N_MOD, d), F32)
    return pl.pallas_call(
        _adaln_kernel,
        grid=(depth, n // tn),
        in_specs=[
            pl.BlockSpec((N_MOD, d), lambda l, j: (0, 0)),
            pl.BlockSpec((None, d, tn), lambda l, j: (l, 0, j)),
            pl.BlockSpec((None, 1, tn), lambda l, j: (l, 0, j)),
        ],
        out_specs=pl.BlockSpec((None, N_MOD, tn), lambda l, j: (l, 0, j)),
        out_shape=jax.ShapeDtypeStruct((depth, N_MOD, n), F32),
        compiler_params=_cparams(("arbitrary", "arbitrary"), est),
        name="adaln",
    )(cvec, w_mod, b_mod.reshape(depth, 1, n))


def _norm_kernel(*refs, modulated):
    if modulated:
        x_ref, g_ref, sc_ref, sh_ref, o_ref = refs
    else:
        x_ref, g_ref, o_ref = refs
    x = x_ref[...]
    y = x * lax.rsqrt(jnp.mean(x * x, axis=-1, keepdims=True) + EPS)
    y = y * g_ref[...]
    if modulated:
        y = y * (1.0 + sc_ref[...]) + sh_ref[...]
    o_ref[...] = y.astype(o_ref.dtype)


def norm_mod(x, g, sc=None, sh=None, *, rows_per_mod=None, out_dtype=BF16, tr=256):
    m, d = x.shape
    modulated = sc is not None
    in_specs = [pl.BlockSpec((tr, d), lambda i: (i, 0)), pl.BlockSpec((1, d), lambda i: (0, 0))]
    args = [x, g.reshape(1, d)]
    if modulated:
        mod_spec = pl.BlockSpec((None, 1, d), lambda i: ((i * tr) // rows_per_mod, 0, 0))
        in_specs += [mod_spec, mod_spec]
        args += [sc, sh]
    est = 2 * _nbytes((tr, d), F32) + 2 * _nbytes((tr, d), out_dtype) + 3 * _nbytes((tr, d), F32)
    return pl.pallas_call(
        functools.partial(_norm_kernel, modulated=modulated),
        grid=(m // tr,),
        in_specs=in_specs,
        out_specs=pl.BlockSpec((tr, d), lambda i: (i, 0)),
        out_shape=jax.ShapeDtypeStruct((m, d), out_dtype),
        compiler_params=_cparams(("arbitrary",), est),
        name="norm_mod",
    )(*args)


def _epilogue(acc, epilogue, r_ref, gate_ref, o_ref):
    if epilogue == "relu2":
        a = jnp.maximum(acc, 0.0)
        o_ref[...] = (a * a).astype(o_ref.dtype)
    elif epilogue == "resid":
        o_ref[...] = (r_ref[...] + gate_ref[...] * acc).astype(o_ref.dtype)
    else:
        o_ref[...] = acc.astype(o_ref.dtype)


def _mm_kernel(*refs, epilogue, nk):
    if epilogue == "resid":
        x_ref, w_ref, r_ref, gate_ref = refs[:4]
        rest = refs[4:]
    else:
        x_ref, w_ref = refs[:2]
        r_ref = gate_ref = None
        rest = refs[2:]
    o_ref = rest[0]

    def part():
        return jnp.dot(x_ref[...], w_ref[...], preferred_element_type=F32)

    if nk == 1:
        _epilogue(part(), epilogue, r_ref, gate_ref, o_ref)
        return
    acc_ref = rest[1]
    k = pl.program_id(2)

    @pl.when(k == 0)
    def _():
        acc_ref[...] = part()

    @pl.when((k > 0) & (k < nk - 1))
    def _():
        acc_ref[...] += part()

    @pl.when(k == nk - 1)
    def _():
        _epilogue(acc_ref[...] + part(), epilogue, r_ref, gate_ref, o_ref)


def matmul(x, w, *, tm, tn, tk=None, epilogue="plain", out_dtype=F32, resid=None, gate=None,
           rows_per_mod=None):
    m, kdim = x.shape
    n = w.shape[1]
    tk = kdim if tk is None else tk
    nk = kdim // tk
    grid = (m // tm, n // tn, nk)
    in_specs = [pl.BlockSpec((tm, tk), lambda i, j, k: (i, k)),
                pl.BlockSpec((tk, tn), lambda i, j, k: (k, j))]
    args = [x, w]
    est = 2 * _nbytes((tm, tk), x.dtype) + 2 * _nbytes((tk, tn), w.dtype)
    est += 2 * _nbytes((tm, tn), out_dtype) + 2 * _nbytes((tm, tn), F32)
    if epilogue == "resid":
        in_specs += [pl.BlockSpec((tm, tn), lambda i, j, k: (i, j)),
                     pl.BlockSpec((None, 1, tn), lambda i, j, k: ((i * tm) // rows_per_mod, 0, j))]
        args += [resid, gate]
        est += 2 * _nbytes((tm, tn), F32)
    scratch = [pltpu.VMEM((tm, tn), F32)] if nk > 1 else []
    return pl.pallas_call(
        functools.partial(_mm_kernel, epilogue=epilogue, nk=nk),
        grid=grid,
        in_specs=in_specs,
        out_specs=pl.BlockSpec((tm, tn), lambda i, j, k: (i, j)),
        out_shape=jax.ShapeDtypeStruct((m, n), out_dtype),
        scratch_shapes=scratch,
        compiler_params=_cparams(("arbitrary", "arbitrary", "arbitrary"), est),
        name="mm_" + epilogue,
    )(*args)


def _rope(x, cos, sin_lo, sin_hi):
    quarter = HEAD_DIM // 4
    return (x * cos + pltpu.roll(x, HEAD_DIM - quarter, 1) * sin_lo + pltpu.roll(x, quarter, 1) * sin_hi)


def _proj_tile_plan(rope, caches):
    heads_per_tile = PROJ_TN // HEAD_DIM
    plan = {}
    for j in range(IN_W // PROJ_TN):
        ops = []
        for hh in range(heads_per_tile):
            col = j * PROJ_TN + hh * HEAD_DIM
            is_bq = COL_BQ <= col < COL_BK
            is_bk = COL_BK <= col < COL_BV
            is_cq = COL_CQ <= col < COL_CK
            ops.append((rope and (is_bq or is_bk), is_bq or is_cq))
        col0 = j * PROJ_TN
        f32_use = None
        if col0 == COL_A:
            f32_use = "a"
        elif caches and col0 == COL_BK:
            f32_use = "win_kv"
        elif caches and col0 == COL_CK:
            f32_use = "na_k"
        elif caches and col0 == COL_CV:
            f32_use = "na_v"
        plan.setdefault((tuple(ops), f32_use), []).append(j)
    return plan


def _proj_kernel(*refs, rope, caches, n_alias):
    refs = list(refs)
    h_ref, w_ref = refs[:2]
    pos = 2
    if rope:
        cos_ref, slo_ref, shi_ref = refs[pos:pos + 3]
        pos += 3
    pos += n_alias
    a32_ref, o16_ref = refs[pos:pos + 2]
    if caches:
        wk_ref, wv_ref, nk_ref, nv_ref = refs[pos + 2:pos + 6]
    j = pl.program_id(1)

    def run(ops, f32_use):
        acc = jnp.dot(h_ref[...], w_ref[...], preferred_element_type=F32)
        if f32_use == "a":
            a32_ref[...] = acc
        elif f32_use == "win_kv":
            wk_ref[...] = acc[:, :WIN_KV_W].reshape(wk_ref.shape)
            wv_ref[...] = acc[:, WIN_KV_W:].reshape(wv_ref.shape)
        elif f32_use == "na_k":
            nk_ref[...] = acc.reshape(nk_ref.shape)
        elif f32_use == "na_v":
            nv_ref[...] = acc.reshape(nv_ref.shape)
        for hh, (do_rope, do_scale) in enumerate(ops):
            cols = slice(hh * HEAD_DIM, (hh + 1) * HEAD_DIM)
            x = acc[:, cols]
            if do_rope:
                x = _rope(x, cos_ref[...], slo_ref[...], shi_ref[...])
            if do_scale:
                x = x * ATTN_SCALE
            o16_ref[:, cols] = x.astype(BF16)

    for (ops, f32_use), tiles in _proj_tile_plan(rope, caches).items():
        cond = j == tiles[0]
        for t in tiles[1:]:
            cond = cond | (j == t)
        pl.when(cond)(functools.partial(run, ops, f32_use))


def in_projection(h, w, rope_tabs=None, *, seq, layer=None, cache_shapes=None, prev_caches=None, tm=1024):
    m, d = h.shape
    n = w.shape[1]
    tn = PROJ_TN
    rope = rope_tabs is not None
    caches = cache_shapes is not None
    in_specs = [pl.BlockSpec((tm, d), lambda i, j: (i, 0)), pl.BlockSpec((d, tn), lambda i, j: (0, j))]
    args = [h, w]
    if rope:
        tiles_per_seq = seq // tm
        tab_spec = pl.BlockSpec((tm, HEAD_DIM), lambda i, j: (i % tiles_per_seq, 0))
        in_specs += [tab_spec] * 3
        args += list(rope_tabs)
    n_alias = 0
    aliases = {}
    if prev_caches is not None:
        n_alias = len(prev_caches)
        for t, arr in enumerate(prev_caches):
            aliases[len(args)] = 2 + t
            in_specs.append(pl.BlockSpec(memory_space=pl.ANY))
            args.append(arr)
    out_specs = [pl.BlockSpec((tm, FNET_W), lambda i, j: (i, 0)), pl.BlockSpec((tm, tn), lambda i, j: (i, j))]
    out_shape = [jax.ShapeDtypeStruct((m, FNET_W), F32), jax.ShapeDtypeStruct((m, n), BF16)]
    est = 2 * _nbytes((tm, d), BF16) + 2 * _nbytes((d, tn), BF16) + 2 * _nbytes((tm, FNET_W), F32)
    est += 2 * _nbytes((tm, tn), BF16) + 2 * _nbytes((tm, tn), F32) + 6 * _nbytes((tm, HEAD_DIM), F32)
    if caches:
        seqs = tm // seq
        for shp in cache_shapes:
            out_specs.append(pl.BlockSpec((seqs, None, seq, shp[3]), lambda i, j: (i, layer, 0, 0)))
            out_shape.append(jax.ShapeDtypeStruct(shp, F32))
            est += 2 * _nbytes((tm, shp[3]), F32)
    return pl.pallas_call(
        functools.partial(_proj_kernel, rope=rope, caches=caches, n_alias=n_alias),
        grid=(m // tm, n // tn),
        in_specs=in_specs,
        out_specs=out_specs,
        out_shape=out_shape,
        input_output_aliases=aliases,
        compiler_params=_cparams(("arbitrary", "arbitrary"), est),
        name="in_proj",
    )(*args)


def _sigmoid(z):
    return 1.0 / (1.0 + jnp.exp(-z))


def _merge_kernel(h_ref, ya_ref, yb_ref, yc_ref, wga_ref, wgb_ref, wgc_ref, bga_ref, bgb_ref, bgc_ref,
                  wa_ref, wb_ref, wc_ref, o_ref):
    h = h_ref[...]

    def branch(wg_ref, bg_ref, y_ref, w_ref):
        gate = _sigmoid(jnp.dot(h, wg_ref[...], preferred_element_type=F32) + bg_ref[...])
        return gate * jnp.dot(y_ref[...], w_ref[...], preferred_element_type=F32)

    m = branch(wga_ref, bga_ref, ya_ref, wa_ref)
    m = m + branch(wgb_ref, bgb_ref, yb_ref, wb_ref)
    m = m + branch(wgc_ref, bgc_ref, yc_ref, wc_ref)
    o_ref[...] = m.astype(o_ref.dtype)


def merge(h, ya, yb, yc, w_gate, b_gate, w_a, w_b, w_c, *, tm=512, tn=256):
    m, d = h.shape
    nj = d // tn
    b_gate = b_gate.reshape(1, 3 * d)
    row = lambda i, j: (i, 0)
    in_specs = [
        pl.BlockSpec((tm, d), row),
        pl.BlockSpec((tm, ya.shape[1]), row),
        pl.BlockSpec((tm, yb.shape[1]), row),
        pl.BlockSpec((tm, yc.shape[1]), row),
        pl.BlockSpec((d, tn), lambda i, j: (0, j)),
        pl.BlockSpec((d, tn), lambda i, j: (0, nj + j)),
        pl.BlockSpec((d, tn), lambda i, j: (0, 2 * nj + j)),
        pl.BlockSpec((1, tn), lambda i, j: (0, j)),
        pl.BlockSpec((1, tn), lambda i, j: (0, nj + j)),
        pl.BlockSpec((1, tn), lambda i, j: (0, 2 * nj + j)),
        pl.BlockSpec((w_a.shape[0], tn), lambda i, j: (0, j)),
        pl.BlockSpec((w_b.shape[0], tn), lambda i, j: (0, j)),
        pl.BlockSpec((w_c.shape[0], tn), lambda i, j: (0, j)),
    ]
    est = 2 * 2 * _nbytes((tm, d), BF16) + 2 * 4 * _nbytes((d, tn), BF16) + 8 * _nbytes((tm, tn), F32)
    return pl.pallas_call(
        _merge_kernel,
        grid=(m // tm, nj),
        in_specs=in_specs,
        out_specs=pl.BlockSpec((tm, tn), lambda i, j: (i, j)),
        out_shape=jax.ShapeDtypeStruct((m, d), BF16),
        compiler_params=_cparams(("arbitrary", "arbitrary"), est),
        name="merge",
    )(h, ya, yb, yc, w_gate, w_gate, w_gate, b_gate, b_gate, b_gate, w_a, w_b, w_c)


def _split_hi_lo(x):
    hi = x.astype(BF16)
    lo = (x - hi.astype(F32)).astype(BF16)
    return hi, lo


def _dot3(xh, xl, wh, wl):
    acc = jnp.dot(xh, wh, preferred_element_type=F32)
    acc += jnp.dot(xl, wh, preferred_element_type=F32)
    acc += jnp.dot(xh, wl, preferred_element_type=F32)
    return acc


def _dot_nt(a, b):
    return lax.dot_general(a, b, (((1,), (1,)), ((), ())), preferred_element_type=F32)


def _row_max(scores, extra=None):
    m = scores[0].max(axis=-1, keepdims=True)
    for s in scores[1:]:
        m = jnp.maximum(m, s.max(axis=-1, keepdims=True))
    if extra is not None:
        m = jnp.maximum(m, extra)
    return m


def _sink_column(sink_ref, hk, rows_per_head):
    rows = WIN_GROUP * rows_per_head
    group = lax.broadcasted_iota(jnp.int32, (rows, 1), 0) // rows_per_head
    sink = jnp.zeros((rows, 1), F32)
    for g in range(WIN_GROUP):
        sink = jnp.where(group == g, sink_ref[hk * WIN_GROUP + g], sink)
    return sink


def _stacked_q(q_refs, hk):
    heads_per_ref = q_refs[0].shape[1] // HEAD_DIM
    parts = []
    for g in range(WIN_GROUP):
        h = hk * WIN_GROUP + g
        ref = q_refs[h // heads_per_ref]
        lo = (h % heads_per_ref) * HEAD_DIM
        parts.append(ref[:, lo:lo + HEAD_DIM])
    return jnp.concatenate(parts, axis=0)


def _ctx_mixer_kernel(a_ref, bq0_ref, bq1_ref, bk_ref, bv_ref, cq_ref, ck_ref, cv_ref, sink_ref,
                      ch_h_ref, ch_l_ref, pc_h_ref, pc_l_ref, ps_h_ref, ps_l_ref,
                      ya_ref, yb_ref, yc_ref, *, seq):
    gw = FNET_GROUP_W
    ch_h = ch_h_ref[...]
    ch_l = ch_l_ref[...]
    pc_h, pc_l, ps_h, ps_l = pc_h_ref[...], pc_l_ref[...], ps_h_ref[...], ps_l_ref[...]
    norm = 1.0 / math.sqrt(seq * gw)
    for g in range(FNET_W // gw):
        a_h, a_l = _split_hi_lo(a_ref[:, g * gw:(g + 1) * gw])
        t = _dot3(a_h, a_l, ch_h, ch_l)
        tc_h, tc_l = _split_hi_lo(t[:, :gw])
        ts_h, ts_l = _split_hi_lo(t[:, gw:])
        y = _dot3(pc_h, pc_l, tc_h, tc_l) - _dot3(ps_h, ps_l, ts_h, ts_l)
        ya_ref[:, g * gw:(g + 1) * gw] = (y * norm).astype(ya_ref.dtype)

    hd = HEAD_DIM
    for hk in range(WIN_KV_HEADS):
        q = _stacked_q((bq0_ref, bq1_ref), hk)
        s = _dot_nt(q, bk_ref[:, hk * hd:(hk + 1) * hd])
        sink = _sink_column(sink_ref, hk, seq)
        m = _row_max([s], sink)
        e = jnp.exp(s - m)
        den = e.sum(axis=-1, keepdims=True) + jnp.exp(sink - m)
        o = jnp.dot(e.astype(BF16), bv_ref[:, hk * hd:(hk + 1) * hd], preferred_element_type=F32) / den
        for g in range(WIN_GROUP):
            h = hk * WIN_GROUP + g
            yb_ref[:, h * hd:(h + 1) * hd] = o[g * seq:(g + 1) * seq].astype(yb_ref.dtype)

    for h in range(NA_HEADS):
        hs = slice(h * hd, (h + 1) * hd)
        s = _dot_nt(cq_ref[:, hs], ck_ref[:, hs])
        m = s.max(axis=-1, keepdims=True)
        e = jnp.exp(s - m)
        den = e.sum(axis=-1, keepdims=True)
        o = jnp.dot(e.astype(BF16), cv_ref[:, hs], preferred_element_type=F32) / den
        yc_ref[:, hs] = o.astype(yc_ref.dtype)


def ctx_mixers(a32, p16, sink, tabs, *, batch, seq):
    def col(width, offset):
        return pl.BlockSpec((seq, width), lambda b: (b, offset // width))

    const = lambda shape: pl.BlockSpec(shape, lambda b: (0,) * len(shape))
    half_q = WIN_Q_W // 2
    in_specs = [
        col(FNET_W, 0), col(half_q, COL_BQ), col(half_q, COL_BQ + half_q),
        col(WIN_KV_W, COL_BK), col(WIN_KV_W, COL_BV),
        col(NA_W, COL_CQ), col(NA_W, COL_CK), col(NA_W, COL_CV),
        pl.BlockSpec(memory_space=pltpu.SMEM),
        const(tabs["ch_h"].shape), const(tabs["ch_l"].shape),
        const(tabs["pc_h"].shape), const(tabs["pc_l"].shape),
        const(tabs["ps_h"].shape), const(tabs["ps_l"].shape),
    ]
    out_specs = [pl.BlockSpec((seq, FNET_W), lambda b: (b, 0)),
                 pl.BlockSpec((seq, WIN_Q_W), lambda b: (b, 0)),
                 pl.BlockSpec((seq, NA_W), lambda b: (b, 0))]
    m = batch * seq
    out_shape = [jax.ShapeDtypeStruct((m, FNET_W), BF16), jax.ShapeDtypeStruct((m, WIN_Q_W), BF16),
                 jax.ShapeDtypeStruct((m, NA_W), BF16)]
    est = 2 * _nbytes((seq, IN_W), F32) + 2 * _nbytes((seq, D_MODEL), BF16) + (16 << 20)
    return pl.pallas_call(
        functools.partial(_ctx_mixer_kernel, seq=seq),
        grid=(batch,),
        in_specs=in_specs,
        out_specs=out_specs,
        out_shape=out_shape,
        compiler_params=_cparams(("arbitrary",), est),
        name="ctx_mixers",
    )(a32, p16, p16, p16, p16, p16, p16, p16, sink,
      tabs["ch_h"], tabs["ch_l"], tabs["pc_h"], tabs["pc_l"], tabs["ps_h"], tabs["ps_l"])


def _group_dot3(x, tab_h, tab_l):
    gw = FNET_GROUP_W
    outs = []
    for g in range(FNET_W // gw):
        x_h, x_l = _split_hi_lo(x[:, g * gw:(g + 1) * gw])
        outs.append(_dot3(x_h, x_l, tab_h, tab_l))
    return jnp.concatenate(outs, axis=1)


def _shifted_reverse(jrev_ref, x_hi, x_lo):
    out = jnp.dot(jrev_ref[...], x_hi, preferred_element_type=F32)
    if x_lo is not None:
        out += jnp.dot(jrev_ref[...], x_lo, preferred_element_type=F32)
    return out


def _dft_fold_kernel(a_ref, am_ref, nb_ref, mid_ref, jrev_ref, ch_h_ref, ch_l_ref, te_ref, to_ref, aux_ref):
    i = pl.program_id(1)
    gw = FNET_GROUP_W
    cc_h, cc_l = ch_h_ref[:, :gw], ch_l_ref[:, :gw]
    sc_h, sc_l = ch_h_ref[:, gw:], ch_l_ref[:, gw:]
    a = a_ref[...]
    tm = a.shape[0]
    row = lax.broadcasted_iota(jnp.int32, (tm, 1), 0)
    am_h, am_l = _split_hi_lo(am_ref[...])
    first = jnp.where(i > 0, nb_ref[0:1, :], 0.0)
    arev = jnp.where(row == 0, first, _shifted_reverse(jrev_ref, am_h, am_l))
    e = a + arev
    o = jnp.where((row == 0) & (i == 0), 0.0, a - arev)

    te_h, te_l = _split_hi_lo(_group_dot3(e, cc_h, cc_l))
    to_h, to_l = _split_hi_lo(_group_dot3(o, sc_h, sc_l))
    te_ref[0], te_ref[1], te_ref[2] = te_h, te_h, te_l
    to_ref[0], to_ref[1], to_ref[2] = to_h, to_h, to_l

    @pl.when(i == 0)
    def _():
        rows8 = lax.broadcasted_iota(jnp.int32, aux_ref.shape, 0)
        aux_ref[...] = _group_dot3(jnp.where(rows8 == 0, mid_ref[0:1, :], 0.0), cc_h, cc_l)


def dft_fold(a32, tabs, *, batch, seq, tm=DFT_FOLD_TM):
    half = seq // 2
    nt = half // tm
    tiles = seq // tm
    sub = 8
    w = FNET_W
    slab = pl.BlockSpec((3, tm, w), lambda b, i: (0, i, b))
    const = lambda arr: pl.BlockSpec(arr.shape, lambda b, i: (0, 0))
    jrev = tabs["jrev_fold"]
    est = 6 * _nbytes((tm, w), F32) + 4 * 3 * _nbytes((tm, w), BF16) + 2 * _nbytes(jrev.shape, BF16) + (12 << 20)
    return pl.pallas_call(
        _dft_fold_kernel,
        grid=(batch, nt),
        in_specs=[pl.BlockSpec((tm, w), lambda b, i: (b * tiles + i, 0)),
                  pl.BlockSpec((tm, w), lambda b, i: (b * tiles + tiles - 1 - i, 0)),
                  pl.BlockSpec((sub, w), lambda b, i: ((b * seq + (seq - i * tm) % seq) // sub, 0)),
                  pl.BlockSpec((sub, w), lambda b, i: ((b * seq + half) // sub, 0)),
                  const(jrev), const(tabs["ch_h"]), const(tabs["ch_l"])],
        out_specs=[slab, slab, pl.BlockSpec((None, sub, w), lambda b, i: (b, 0, 0))],
        out_shape=[jax.ShapeDtypeStruct((3, half, batch * w), BF16),
                   jax.ShapeDtypeStruct((3, half, batch * w), BF16),
                   jax.ShapeDtypeStruct((batch, sub, w), F32)],
        compiler_params=_cparams(("arbitrary", "arbitrary"), est),
        name="dft_fold",
    )(a32, a32, a32, a32, jrev, tabs["ch_h"], tabs["ch_l"])


def _dft_pos_kernel(c3_ref, s3_ref, c3x_ref, s3x_ref, te_ref, to_ref, aux_ref, jrev_ref, lo_ref, hi_ref,
                    accc_ref, accs_ref, accx_ref, *, nk, norm):
    k = pl.program_id(2)

    def dots():
        te, to = te_ref[...], to_ref[...]
        return (jnp.dot(c3_ref[...], te, preferred_element_type=F32),
                jnp.dot(s3_ref[...], to, preferred_element_type=F32),
                jnp.dot(c3x_ref[...], te, preferred_element_type=F32)
                + jnp.dot(s3x_ref[...], to, preferred_element_type=F32))

    @pl.when(k == 0)
    def _():
        accc_ref[...], accs_ref[...], accx_ref[...] = dots()

    @pl.when((k > 0) & (k < nk - 1))
    def _():
        dc, ds, dx = dots()
        accc_ref[...] += dc
        accs_ref[...] += ds
        accx_ref[...] += dx

    @pl.when(k == nk - 1)
    def _():
        dc, ds, dx = dots()
        tm = dc.shape[0]
        row = lax.broadcasted_iota(jnp.int32, (tm, 1), 0)
        mid_c = aux_ref[0:1, :]
        pc = accc_ref[...] + dc + jnp.where(row % 2 == 0, mid_c, -mid_c)
        ps = accs_ref[...] + ds
        lo_ref[...] = ((pc - ps) * norm).astype(lo_ref.dtype)
        mirrored = _shifted_reverse(jrev_ref, ((pc + ps) * norm).astype(BF16), None)
        edge = ((accx_ref[...] + dx)[0:1, :] + mid_c) * norm
        hi_ref[...] = jnp.where(row == 0, edge, mirrored).astype(hi_ref.dtype)


def dft_pos(te, to, aux, tabs, *, batch, seq, tm=DFT_POS_TM, tk=1024):
    half = seq // 2
    w = FNET_W
    kdim = 3 * half
    nk = kdim // tk
    ni = half // tm
    sub = 8
    norm = 1.0 / math.sqrt(seq * FNET_GROUP_W)
    jrev = tabs["jrev_pos"]
    tab_spec = pl.BlockSpec((tm, tk), lambda i, j, k: (i, k))
    edge_spec = pl.BlockSpec((None, sub, tk), lambda i, j, k: (i, 0, k))
    slab_spec = pl.BlockSpec((tk, w), lambda i, j, k: (k, j))
    est = 4 * _nbytes((tm, tk), BF16) + 4 * _nbytes((tk, w), BF16) + 5 * _nbytes((tm, w), F32)
    est += 4 * _nbytes((tm, w), BF16) + 2 * _nbytes(jrev.shape, BF16) + (2 << 20)
    return pl.pallas_call(
        functools.partial(_dft_pos_kernel, nk=nk, norm=norm),
        grid=(ni, batch, nk),
        in_specs=[tab_spec, tab_spec, edge_spec, edge_spec, slab_spec, slab_spec,
                  pl.BlockSpec((None,) + aux.shape[1:], lambda i, j, k: (j, 0, 0)),
                  pl.BlockSpec(jrev.shape, lambda i, j, k: (0, 0))],
        out_specs=[pl.BlockSpec((None, tm, w), lambda i, j, k: (j, i, 0)),
                   pl.BlockSpec((None, tm, w), lambda i, j, k: (j, ni - 1 - i, 0))],
        out_shape=[jax.ShapeDtypeStruct((batch, half, w), BF16)] * 2,
        scratch_shapes=[pltpu.VMEM((tm, w), F32), pltpu.VMEM((tm, w), F32), pltpu.VMEM((sub, w), F32)],
        compiler_params=_cparams(("arbitrary", "arbitrary", "arbitrary"), est),
        name="dft_pos",
    )(tabs["c3"], tabs["s3"], tabs["c3x"], tabs["s3x"],
      te.reshape(kdim, batch * w), to.reshape(kdim, batch * w), aux, jrev)


def latent_fourier(a32, tabs, *, batch, seq):
    te, to, aux = dft_fold(a32, tabs, batch=batch, seq=seq)
    lo, hi = dft_pos(te, to, aux, tabs, batch=batch, seq=seq)
    return jnp.concatenate([lo, hi], axis=1).reshape(batch * seq, FNET_W)


def _win_attn_kernel(q0_ref, q1_ref, k_ref, v_ref, ck_ref, cv_ref, sink_ref, o_ref, *, seq):
    n = pl.program_id(1)
    blk = WIN
    span = 3 * blk
    hd = HEAD_DIM
    start = pl.multiple_of(jnp.clip((n - 1) * blk, 0, seq - span), blk)
    rows = WIN_GROUP * blk
    ipos = n * blk + lax.broadcasted_iota(jnp.int32, (rows, span), 0) % blk
    jpos = start + lax.broadcasted_iota(jnp.int32, (rows, span), 1)
    valid = jnp.abs(ipos - jpos) <= WIN

    for hk in range(WIN_KV_HEADS):
        hs = slice(hk * hd, (hk + 1) * hd)
        q = _stacked_q((q0_ref, q1_ref), hk)
        s = jnp.where(valid, _dot_nt(q, k_ref[pl.ds(start, span), hs]), NEG_INF)
        sc = _dot_nt(q, ck_ref[:, hs].astype(BF16))
        sink = _sink_column(sink_ref, hk, blk)
        m = _row_max([s, sc], sink)
        e = jnp.exp(s - m)
        ec = jnp.exp(sc - m)
        den = e.sum(axis=-1, keepdims=True) + ec.sum(axis=-1, keepdims=True) + jnp.exp(sink - m)
        o = jnp.dot(e.astype(BF16), v_ref[pl.ds(start, span), hs], preferred_element_type=F32)
        o += jnp.dot(ec.astype(BF16), cv_ref[:, hs].astype(BF16), preferred_element_type=F32)
        o = o / den
        for g in range(WIN_GROUP):
            h = hk * WIN_GROUP + g
            o_ref[:, h * hd:(h + 1) * hd] = o[g * blk:(g + 1) * blk].astype(o_ref.dtype)


def win_attention(p16, cache_k, cache_v, sink, *, batch, seq, layer):
    blk = WIN
    nblk = seq // blk
    past = cache_k.shape[2]
    half_q = WIN_Q_W // 2

    def q_spec(offset):
        return pl.BlockSpec((blk, half_q), lambda b, n: (b * nblk + n, offset // half_q))

    def seq_spec(offset):
        return pl.BlockSpec((seq, WIN_KV_W), lambda b, n: (b, offset // WIN_KV_W))

    cache_spec = pl.BlockSpec((None, None, past, WIN_KV_W), lambda b, n: (b, layer, 0, 0))
    in_specs = [q_spec(COL_BQ), q_spec(COL_BQ + half_q), seq_spec(COL_BK), seq_spec(COL_BV),
                cache_spec, cache_spec, pl.BlockSpec(memory_space=pltpu.SMEM)]
    est = 4 * _nbytes((seq, WIN_KV_W), BF16) + 4 * _nbytes((past, WIN_KV_W), F32) + (16 << 20)
    return pl.pallas_call(
        functools.partial(_win_attn_kernel, seq=seq),
        grid=(batch, nblk),
        in_specs=in_specs,
        out_specs=pl.BlockSpec((blk, WIN_Q_W), lambda b, n: (b * nblk + n, 0)),
        out_shape=jax.ShapeDtypeStruct((batch * seq, WIN_Q_W), BF16),
        compiler_params=_cparams(("arbitrary", "arbitrary"), est),
        name="win_attn",
    )(p16, p16, p16, p16, cache_k, cache_v, sink)


def _na_window_start(rg, rows_n):
    return jnp.clip(rg * NA_QROWS - NA_KH // 2, 0, rows_n - NA_KROWS)


def _na_attn_kernel(q_ref, k_ref, v_ref, ck_ref, cv_ref, bias_ref, o_ref, *, rows_n):
    rg = pl.program_id(1)
    hd = HEAD_DIM
    nkeys = NA_KROWS * GRID_W
    start = pl.multiple_of(_na_window_start(rg, rows_n) * GRID_W, GRID_W)
    for h in range(NA_HEADS):
        hs = slice(h * hd, (h + 1) * hd)
        q = q_ref[:, hs]
        s = _dot_nt(q, k_ref[pl.ds(start, nkeys), hs]) + bias_ref[h]
        sc = _dot_nt(q, ck_ref[:, hs].astype(BF16))
        m = _row_max([s, sc])
        e = jnp.exp(s - m)
        ec = jnp.exp(sc - m)
        den = e.sum(axis=-1, keepdims=True) + ec.sum(axis=-1, keepdims=True)
        o = jnp.dot(e.astype(BF16), v_ref[pl.ds(start, nkeys), hs], preferred_element_type=F32)
        o += jnp.dot(ec.astype(BF16), cv_ref[:, hs].astype(BF16), preferred_element_type=F32)
        o_ref[:, hs] = (o / den).astype(o_ref.dtype)


def na_attention(p16, cache_k, cache_v, bias_tab, *, batch, seq, layer):
    rows_n = seq // GRID_W
    ngroups = rows_n // NA_QROWS
    nq = NA_QROWS * GRID_W
    nkeys = NA_KROWS * GRID_W
    past = cache_k.shape[2]

    def seq_spec(offset):
        return pl.BlockSpec((seq, NA_W), lambda b, rg: (b, offset // NA_W), pipeline_mode=pl.Buffered(1))

    def pattern(b, rg):
        return (layer, (rg > 0).astype(jnp.int32) + (rg == ngroups - 1).astype(jnp.int32), 0, 0, 0)

    cache_spec = pl.BlockSpec((None, None, past, NA_W), lambda b, rg: (b, layer, 0, 0))
    in_specs = [pl.BlockSpec((nq, NA_W), lambda b, rg: (b * ngroups + rg, COL_CQ // NA_W)),
                seq_spec(COL_CK), seq_spec(COL_CV), cache_spec, cache_spec,
                pl.BlockSpec((None, None, NA_HEADS, nq, nkeys), pattern, pipeline_mode=pl.Buffered(1))]
    est = 2 * _nbytes((seq, NA_W), BF16) + _nbytes((NA_HEADS, nq, nkeys), F32)
    est += 4 * _nbytes((past, NA_W), F32) + (12 << 20)
    return pl.pallas_call(
        functools.partial(_na_attn_kernel, rows_n=rows_n),
        grid=(batch, ngroups),
        in_specs=in_specs,
        out_specs=pl.BlockSpec((nq, NA_W), lambda b, rg: (b * ngroups + rg, 0)),
        out_shape=jax.ShapeDtypeStruct((batch * seq, NA_W), BF16),
        compiler_params=_cparams(("arbitrary", "arbitrary"), est),
        name="na_attn",
    )(p16, p16, p16, cache_k, cache_v, bias_tab)


def _dft_cos_sin(n):
    idx = jnp.arange(n, dtype=jnp.int32)
    phase = (idx[:, None] * idx[None, :]) % n
    ang = phase.astype(F32) * (2.0 * math.pi / n)
    return jnp.cos(ang), jnp.sin(ang)


def _hi_lo(x):
    hi = x.astype(BF16)
    return hi, (x - hi.astype(F32)).astype(BF16)


def _fourier_tables(ctx_seq, lat_seq):
    cc, sc = _dft_cos_sin(FNET_GROUP_W)
    ch_h, ch_l = _hi_lo(jnp.concatenate([cc, sc], axis=1))
    cp, sp = _dft_cos_sin(ctx_seq)
    pc_h, pc_l = _hi_lo(cp)
    ps_h, ps_l = _hi_lo(sp)
    half = lat_seq // 2
    idx = jnp.arange(half, dtype=jnp.int32)

    def tables(rows):
        ang = ((rows[:, None] * idx[None, :]) % lat_seq).astype(F32) * (2.0 * math.pi / lat_seq)
        c_h, c_l = _hi_lo(jnp.cos(ang))
        s_h, s_l = _hi_lo(jnp.sin(ang))
        return jnp.concatenate([c_h, c_l, c_h], axis=1), jnp.concatenate([s_h, s_l, s_h], axis=1)

    c3, s3 = tables(idx)
    ni = half // DFT_POS_TM
    c3x, s3x = tables((jnp.arange(ni, dtype=jnp.int32) + 1) * DFT_POS_TM)
    pad = lambda t: jnp.pad(t[:, None, :], ((0, 0), (0, 7), (0, 0)))
    return dict(ch_h=ch_h, ch_l=ch_l, pc_h=pc_h, pc_l=pc_l, ps_h=ps_h, ps_l=ps_l, c3=c3, s3=s3,
                c3x=pad(c3x), s3x=pad(s3x),
                jrev_fold=_shifted_antidiagonal(DFT_FOLD_TM), jrev_pos=_shifted_antidiagonal(DFT_POS_TM))


def _shifted_antidiagonal(n):
    r = jnp.arange(n)
    return ((r[:, None] + r[None, :]) == n).astype(BF16)


def _rope_tables(seq):
    t = jnp.arange(seq)
    n_freq = HEAD_DIM // 4
    inv = ROPE_BASE ** (-jnp.arange(n_freq, dtype=F32) / n_freq)
    ang_r = (t // GRID_W).astype(F32)[:, None] * inv
    ang_c = (t % GRID_W).astype(F32)[:, None] * inv
    zeros = jnp.zeros_like(ang_r)
    cos = jnp.concatenate([jnp.cos(ang_r)] * 2 + [jnp.cos(ang_c)] * 2, axis=1)
    sin_lo = jnp.concatenate([-jnp.sin(ang_r), zeros, -jnp.sin(ang_c), zeros], axis=1)
    sin_hi = jnp.concatenate([zeros, jnp.sin(ang_r), zeros, jnp.sin(ang_c)], axis=1)
    return cos, sin_lo, sin_hi


def _na_bias_tables(rpb, rows_n):
    depth = rpb.shape[0]
    col = jnp.arange(GRID_W)
    cs = jnp.clip(col - NA_KW // 2, 0, GRID_W - NA_KW)
    col_ok = (col[None, :] >= cs[:, None]) & (col[None, :] < cs[:, None] + NA_KW)
    idx_c = jnp.clip(col[None, :] - col[:, None], -(NA_KW - 1), NA_KW - 1) + NA_KW - 1
    bt = jnp.where(col_ok[None, None, None], rpb.astype(F32)[:, :, :, idx_c], NEG_INF)

    ngroups = rows_n // NA_QROWS
    groups = jnp.array([0, 1, ngroups - 1])
    r = groups[:, None] * NA_QROWS + jnp.arange(NA_QROWS)[None, :]
    ws = jnp.clip(groups * NA_QROWS - NA_KH // 2, 0, rows_n - NA_KROWS)
    krow = ws[:, None] + jnp.arange(NA_KROWS)[None, :]
    rs = jnp.clip(r - NA_KH // 2, 0, rows_n - NA_KH)
    row_ok = (krow[:, None, :] >= rs[:, :, None]) & (krow[:, None, :] < rs[:, :, None] + NA_KH)
    rho = jnp.clip(krow[:, None, :] - r[:, :, None] + NA_KH - 1, 0, 2 * NA_KH - 2)
    tab = jnp.where(row_ok[None, None, :, :, :, None, None], bt[:, :, rho], NEG_INF)
    tab = jnp.transpose(tab, (0, 2, 1, 3, 5, 4, 6))
    return tab.reshape(depth, 3, NA_HEADS, NA_QROWS * GRID_W, NA_KROWS * GRID_W)


def kernel(x_prompt, x_sample, cache_win_k, cache_win_v, cache_na_k, cache_na_v, c, c_ctx, w_mod, b_mod,
           norm1_g, w_in, win_sink, na_rpb, w_a, w_b, w_c, w_gate, b_gate, w_out, norm2_g, w1, w2, final_g):
    batch, seq, d = x_prompt.shape
    dec_batch, dec_seq, _ = x_sample.shape
    past = cache_win_k.shape[2]
    m_ctx, m_lat = batch * seq, dec_batch * dec_seq

    w_in_b = w_in.astype(BF16)
    w_a_b, w_b_b, w_c_b = w_a.astype(BF16), w_b.astype(BF16), w_c.astype(BF16)
    w_gate_b, w_out_b = w_gate.astype(BF16), w_out.astype(BF16)
    w1_b, w2_b = w1.astype(BF16), w2.astype(BF16)

    cvec = jnp.concatenate([c_ctx[None], c, jnp.zeros((N_MOD - 1 - dec_batch, d), F32)], axis=0)
    mods = adaln_all(cvec, w_mod, b_mod).reshape(DEPTH, N_MOD, 6, 1, d)

    ftabs = _fourier_tables(seq, dec_seq)
    rope_tabs = _rope_tables(dec_seq)
    na_bias = _na_bias_tables(na_rpb, dec_seq // GRID_W)
    cwk, cwv = (t.reshape(dec_batch, DEPTH, past, WIN_KV_W) for t in (cache_win_k, cache_win_v))
    cnk, cnv = (t.reshape(dec_batch, DEPTH, past, NA_W) for t in (cache_na_k, cache_na_v))
    cache_shapes = [(batch, DEPTH, seq, WIN_KV_W)] * 2 + [(batch, DEPTH, seq, NA_W)] * 2

    xp = x_prompt.reshape(m_ctx, d)
    xs = x_sample.reshape(m_lat, d)
    groups = (
        dict(rows_per_mod=m_ctx, mod=slice(0, 1)),
        dict(rows_per_mod=dec_seq, mod=slice(1, 1 + dec_batch)),
    )
    new_caches = None
    for l in range(DEPTH):
        new_x = []
        for gi, (x, grp) in enumerate(zip((xp, xs), groups)):
            rpm = grp["rows_per_mod"]
            sh1, sc1, gt1, sh2, sc2, gt2 = [mods[l, grp["mod"], k] for k in range(6)]
            h = norm_mod(x, norm1_g[l], sc1, sh1, rows_per_mod=rpm)
            if gi == 0:
                a32, p16, *new_caches = in_projection(h, w_in_b[l], seq=seq, layer=l, cache_shapes=cache_shapes,
                                                      prev_caches=new_caches, tm=512)
                ya, yb, yc = ctx_mixers(a32, p16, win_sink[l], ftabs, batch=batch, seq=seq)
            else:
                a32, p16 = in_projection(h, w_in_b[l], rope_tabs, seq=dec_seq)
                ya = latent_fourier(a32, ftabs, batch=dec_batch, seq=dec_seq)
                yb = win_attention(p16, cwk, cwv, win_sink[l], batch=dec_batch, seq=dec_seq, layer=l)
                yc = na_attention(p16, cnk, cnv, na_bias, batch=dec_batch, seq=dec_seq, layer=l)
            mrg = merge(h, ya, yb, yc, w_gate_b[l], b_gate[l], w_a_b[l], w_b_b[l], w_c_b[l])
            x = matmul(mrg, w_out_b[l], tm=1024, tn=1024, epilogue="resid", resid=x, gate=gt1,
                       rows_per_mod=rpm)
            h2 = norm_mod(x, norm2_g[l], sc2, sh2, rows_per_mod=rpm)
            u = matmul(h2, w1_b[l], tm=1024, tn=1024, epilogue="relu2", out_dtype=BF16)
            x = matmul(u, w2_b[l], tm=1024, tn=1024, tk=2048, epilogue="resid", resid=x, gate=gt2,
                       rows_per_mod=rpm)
            new_x.append(x)
        xp, xs = new_x

    y_prompt = norm_mod(xp, final_g, out_dtype=F32).reshape(batch, seq, d)
    y_sample = norm_mod(xs, final_g, out_dtype=F32).reshape(dec_batch, dec_seq, d)
    wk, wv, nk, nv = new_caches
    kv_shape = lambda heads: (batch, DEPTH, seq, heads, HEAD_DIM)
    return (y_prompt, y_sample, wk.reshape(kv_shape(WIN_KV_HEADS)), wv.reshape(kv_shape(WIN_KV_HEADS)),
            nk.reshape(kv_shape(NA_HEADS)), nv.reshape(kv_shape(NA_HEADS)))
```

```python
import functools
import math

import jax
import jax.numpy as jnp
from jax import lax
from jax.experimental import pallas as pl
from jax.experimental.pallas import tpu as pltpu

F32 = jnp.float32
BF16 = jnp.bfloat16

D_MODEL = 4096
DEPTH = 4
GRID_W = 64
HEAD_DIM = 128
FNET_GROUP_W = 256
FNET_W = 1024
WIN_Q_HEADS = 16
WIN_KV_HEADS = 4
WIN_GROUP = WIN_Q_HEADS // WIN_KV_HEADS
WIN = 128
WIN_Q_W = WIN_Q_HEADS * HEAD_DIM
WIN_KV_W = WIN_KV_HEADS * HEAD_DIM
NA_HEADS = 8
NA_KH = 8
NA_KW = 16
NA_W = NA_HEADS * HEAD_DIM
MLP_HIDDEN = 4 * D_MODEL
IN_W = 7168
ROPE_BASE = 10000.0
EPS = 1e-6
NEG_INF = -1e30
ATTN_SCALE = HEAD_DIM ** -0.5
N_MOD = 8
MOD_PAD_ROWS = 16

COL_A, COL_BQ, COL_BK, COL_BV, COL_CQ, COL_CK, COL_CV = 0, 1024, 3072, 3584, 4096, 5120, 6144
PROJ_TN = 1024

NA_QROWS = 4
NA_KROWS = NA_QROWS + NA_KH

DFT_FOLD_TM = 512
DFT_POS_TM = 1024

V7X_SCOPED_VMEM_BYTES = 60000 * 1024
VMEM_MARGIN_BYTES = 6 << 20


def _cparams(semantics, est_bytes):
    limit = min(V7X_SCOPED_VMEM_BYTES, int(est_bytes) + VMEM_MARGIN_BYTES)
    return pltpu.CompilerParams(dimension_semantics=semantics, vmem_limit_bytes=limit)


def _nbytes(shape, dtype):
    return math.prod(shape) * jnp.dtype(dtype).itemsize


def _adaln_kernel(c_ref, w_ref, b_ref, o_ref):
    c = c_ref[...]
    s = (c * (1.0 / (1.0 + jnp.exp(-c)))).astype(BF16)
    o_ref[...] = jnp.dot(s, w_ref[...].astype(BF16), preferred_element_type=F32) + b_ref[...]


def adaln_all(cvec, w_mod, b_mod, tn=512):
    depth, d, n = w_mod.shape
    est = 2 * _nbytes((d, tn), F32) + _nbytes((d, tn), BF16) + 4 * _nbytes((N_MOD, d), F32)
    return pl.pallas_call(
        _adaln_kernel,
        grid=(depth, n // tn),
        in_specs=[
            pl.BlockSpec((N_MOD, d), lambda l, j: (0, 0)),
            pl.BlockSpec((None, d, tn), lambda l, j: (l, 0, j)),
            pl.BlockSpec((None, 1, tn), lambda l, j: (l, 0, j)),
        ],
        out_specs=pl.BlockSpec((None, N_MOD, tn), lambda l, j: (l, 0, j)),
        out_shape=jax.ShapeDtypeStruct((depth, N_MOD, n), F32),
        compiler_params=_cparams(("arbitrary", "arbitrary"), est),
        name="adaln",
    )(cvec, w_mod, b_mod.reshape(depth, 1, n))


SSQ_LANES = 128


class ModRows:
    def __init__(self, offset, rows_per_mod):
        self.offset = offset
        self.rows_per_mod = rows_per_mod

    def index(self, i, tm):
        return self.offset + (i * tm) // self.rows_per_mod


def _row_scale(ssq_ref, d):
    return lax.rsqrt(ssq_ref[:, 0:1] * (1.0 / d) + EPS)


def _row_ssq(x, shape):
    return jnp.broadcast_to(jnp.sum(x * x, axis=-1, keepdims=True), shape)


def _prep_kernel(x_ref, g_ref, sc_ref, xg_ref, ssq_ref):
    x = x_ref[...]
    xg_ref[...] = (x * (g_ref[...] * (1.0 + sc_ref[...]))).astype(xg_ref.dtype)
    ssq_ref[...] = _row_ssq(x, ssq_ref.shape)


def prep_norm(x, g, sc, mod, *, tr=256):
    m, d = x.shape
    est = 2 * _nbytes((tr, d), F32) + 2 * _nbytes((tr, d), BF16) + 3 * _nbytes((tr, d), F32)
    return pl.pallas_call(
        _prep_kernel,
        grid=(m // tr,),
        in_specs=[pl.BlockSpec((tr, d), lambda i: (i, 0)),
                  pl.BlockSpec((1, d), lambda i: (0, 0)),
                  pl.BlockSpec((None, 1, d), lambda i: (mod.index(i, tr), 0, 0))],
        out_specs=[pl.BlockSpec((tr, d), lambda i: (i, 0)), pl.BlockSpec((tr, SSQ_LANES), lambda i: (i, 0))],
        out_shape=[jax.ShapeDtypeStruct((m, d), BF16), jax.ShapeDtypeStruct((m, SSQ_LANES), F32)],
        compiler_params=_cparams(("arbitrary",), est),
        name="prep_norm",
    )(x, g.reshape(1, d), sc)


def _final_norm_kernel(x_ref, g_ref, o_ref):
    x = x_ref[...]
    y = x * lax.rsqrt(jnp.mean(x * x, axis=-1, keepdims=True) + EPS)
    o_ref[...] = (y * g_ref[...]).astype(o_ref.dtype)


def final_norm(x, g, *, tr=256):
    m, d = x.shape
    est = 4 * _nbytes((tr, d), F32) + 3 * _nbytes((tr, d), F32)
    return pl.pallas_call(
        _final_norm_kernel,
        grid=(m // tr,),
        in_specs=[pl.BlockSpec((tr, d), lambda i: (i, 0)), pl.BlockSpec((1, d), lambda i: (0, 0))],
        out_specs=pl.BlockSpec((tr, d), lambda i: (i, 0)),
        out_shape=jax.ShapeDtypeStruct((m, d), F32),
        compiler_params=_cparams(("arbitrary",), est),
        name="final_norm",
    )(x, g.reshape(1, d))


def _mod_bias_kernel(sh_ref, w_ref, o_ref):
    o_ref[...] = jnp.dot(sh_ref[...], w_ref[...], preferred_element_type=F32)


def mod_bias(sh, w, layer, *, tn=1024):
    rows, d = sh.shape
    n = w.shape[2]
    est = 2 * _nbytes((d, tn), BF16) + 4 * _nbytes((rows, d), BF16) + 4 * _nbytes((rows, tn), F32)
    out = pl.pallas_call(
        _mod_bias_kernel,
        grid=(n // tn,),
        in_specs=[pl.BlockSpec((rows, d), lambda j: (0, 0)),
                  pl.BlockSpec((None, d, tn), lambda j: (layer, 0, j))],
        out_specs=pl.BlockSpec((rows, tn), lambda j: (0, j)),
        out_shape=jax.ShapeDtypeStruct((rows, n), F32),
        compiler_params=_cparams(("arbitrary",), est),
        name="mod_bias",
    )(sh, w)
    return out.reshape(rows, 1, n)


def _mlp_up_kernel(xg_ref, w_ref, ssq_ref, shw_ref, o_ref):
    acc = jnp.dot(xg_ref[...], w_ref[...], preferred_element_type=F32)
    a = jnp.maximum(_row_scale(ssq_ref, xg_ref.shape[1]) * acc + shw_ref[...], 0.0)
    o_ref[...] = (a * a).astype(o_ref.dtype)


def mlp_up(xg, ssq, w, shw, layer, mod, *, tm=1024, tn=1024):
    m, d = xg.shape
    n = w.shape[2]
    est = 2 * _nbytes((tm, d), BF16) + 2 * _nbytes((d, tn), BF16) + 2 * _nbytes((tm, tn), BF16)
    est += 2 * _nbytes((tm, tn), F32) + 2 * _nbytes((tm, SSQ_LANES), F32)
    return pl.pallas_call(
        _mlp_up_kernel,
        grid=(m // tm, n // tn),
        in_specs=[pl.BlockSpec((tm, d), lambda i, j: (i, 0)),
                  pl.BlockSpec((None, d, tn), lambda i, j: (layer, 0, j)),
                  pl.BlockSpec((tm, SSQ_LANES), lambda i, j: (i, 0)),
                  pl.BlockSpec((None, 1, tn), lambda i, j: (mod.index(i, tm), 0, j))],
        out_specs=pl.BlockSpec((tm, tn), lambda i, j: (i, j)),
        out_shape=jax.ShapeDtypeStruct((m, n), BF16),
        compiler_params=_cparams(("arbitrary", "arbitrary"), est),
        name="mlp_up",
    )(xg, w, ssq, shw)


def _mm_resid_kernel(*refs, nk, emit):
    x_ref, w_ref, r_ref, gate_ref = refs[:4]
    pos = 4
    if emit:
        gn_ref, scn_ref = refs[pos:pos + 2]
        pos += 2
    o_ref = refs[pos]
    pos += 1
    if emit:
        xg_ref, ssq_ref = refs[pos:pos + 2]
        pos += 2
    j = pl.program_id(1)

    def part():
        return jnp.dot(x_ref[...], w_ref[...], preferred_element_type=F32)

    def finish(acc):
        xn = r_ref[...] + gate_ref[...] * acc
        o_ref[...] = xn
        if emit:
            xg_ref[...] = (xn * (gn_ref[...] * (1.0 + scn_ref[...]))).astype(xg_ref.dtype)
            ssq = _row_ssq(xn, ssq_ref.shape)

            @pl.when(j == 0)
            def _():
                ssq_ref[...] = ssq

            @pl.when(j > 0)
            def _():
                ssq_ref[...] += ssq

    if nk == 1:
        finish(part())
        return
    acc_ref = refs[pos]
    k = pl.program_id(2)

    @pl.when(k == 0)
    def _():
        acc_ref[...] = part()

    @pl.when((k > 0) & (k < nk - 1))
    def _():
        acc_ref[...] += part()

    @pl.when(k == nk - 1)
    def _():
        finish(acc_ref[...] + part())


def matmul_resid(x, w, layer, resid, gate, mod, *, next_norm=None, tm=1024, tn=1024, tk=None):
    m, kdim = x.shape
    n = w.shape[2]
    tk = kdim if tk is None else tk
    nk = kdim // tk
    emit = next_norm is not None
    mod_spec = pl.BlockSpec((None, 1, tn), lambda i, j, k: (mod.index(i, tm), 0, j))
    tile_spec = pl.BlockSpec((tm, tn), lambda i, j, k: (i, j))
    in_specs = [pl.BlockSpec((tm, tk), lambda i, j, k: (i, k)),
                pl.BlockSpec((None, tk, tn), lambda i, j, k: (layer, k, j)),
                tile_spec, mod_spec]
    args = [x, w, resid, gate]
    out_specs = [tile_spec]
    out_shape = [jax.ShapeDtypeStruct((m, n), F32)]
    est = 2 * _nbytes((tm, tk), BF16) + 2 * _nbytes((tk, tn), BF16) + 6 * _nbytes((tm, tn), F32)
    if emit:
        g, sc = next_norm
        in_specs += [pl.BlockSpec((1, tn), lambda i, j, k: (0, j)), mod_spec]
        args += [g.reshape(1, n), sc]
        out_specs += [tile_spec, pl.BlockSpec((tm, SSQ_LANES), lambda i, j, k: (i, 0))]
        out_shape += [jax.ShapeDtypeStruct((m, n), BF16), jax.ShapeDtypeStruct((m, SSQ_LANES), F32)]
        est += 2 * _nbytes((tm, tn), BF16) + 2 * _nbytes((tm, SSQ_LANES), F32)
    if nk > 1:
        est += _nbytes((tm, tn), F32)
    outs = pl.pallas_call(
        functools.partial(_mm_resid_kernel, nk=nk, emit=emit),
        grid=(m // tm, n // tn, nk),
        in_specs=in_specs,
        out_specs=out_specs,
        out_shape=out_shape,
        scratch_shapes=[pltpu.VMEM((tm, tn), F32)] if nk > 1 else [],
        compiler_params=_cparams(("arbitrary", "arbitrary", "arbitrary"), est),
        name="mm_resid",
    )(*args)
    return outs if emit else outs[0]


def _rope(x, cos, sin_lo, sin_hi):
    quarter = HEAD_DIM // 4
    return (x * cos + pltpu.roll(x, HEAD_DIM - quarter, 1) * sin_lo + pltpu.roll(x, quarter, 1) * sin_hi)


def _proj_tile_plan(rope, caches):
    heads_per_tile = PROJ_TN // HEAD_DIM
    plan = {}
    for j in range(IN_W // PROJ_TN):
        ops = []
        for hh in range(heads_per_tile):
            col = j * PROJ_TN + hh * HEAD_DIM
            is_bq = COL_BQ <= col < COL_BK
            is_bk = COL_BK <= col < COL_BV
            is_cq = COL_CQ <= col < COL_CK
            ops.append((rope and (is_bq or is_bk), is_bq or is_cq))
        col0 = j * PROJ_TN
        f32_use = None
        if col0 == COL_A:
            f32_use = "a"
        elif caches and col0 == COL_BK:
            f32_use = "win_kv"
        elif caches and col0 == COL_CK:
            f32_use = "na_k"
        elif caches and col0 == COL_CV:
            f32_use = "na_v"
        plan.setdefault((tuple(ops), f32_use), []).append(j)
    return plan


def _proj_kernel(*refs, rope, caches, n_alias):
    refs = list(refs)
    xg_ref, w_ref, ssq_ref, shw_ref = refs[:4]
    pos = 4
    if rope:
        cos_ref, slo_ref, shi_ref = refs[pos:pos + 3]
        pos += 3
    pos += n_alias
    a32_ref, o16_ref = refs[pos:pos + 2]
    if caches:
        wk_ref, wv_ref, nk_ref, nv_ref = refs[pos + 2:pos + 6]
    j = pl.program_id(1)

    def run(ops, f32_use):
        acc = jnp.dot(xg_ref[...], w_ref[...], preferred_element_type=F32)
        acc = _row_scale(ssq_ref, xg_ref.shape[1]) * acc + shw_ref[...]
        if f32_use == "a":
            a32_ref[...] = acc
        elif f32_use == "win_kv":
            wk_ref[...] = acc[:, :WIN_KV_W].reshape(wk_ref.shape)
            wv_ref[...] = acc[:, WIN_KV_W:].reshape(wv_ref.shape)
        elif f32_use == "na_k":
            nk_ref[...] = acc.reshape(nk_ref.shape)
        elif f32_use == "na_v":
            nv_ref[...] = acc.reshape(nv_ref.shape)
        for hh, (do_rope, do_scale) in enumerate(ops):
            cols = slice(hh * HEAD_DIM, (hh + 1) * HEAD_DIM)
            x = acc[:, cols]
            if do_rope:
                x = _rope(x, cos_ref[...], slo_ref[...], shi_ref[...])
            if do_scale:
                x = x * ATTN_SCALE
            o16_ref[:, cols] = x.astype(BF16)

    for (ops, f32_use), tiles in _proj_tile_plan(rope, caches).items():
        cond = j == tiles[0]
        for t in tiles[1:]:
            cond = cond | (j == t)
        pl.when(cond)(functools.partial(run, ops, f32_use))


def in_projection(xg, ssq, w, shw, layer, mod, rope_tabs=None, *, seq, cache_shapes=None, prev_caches=None,
                  tm=1024):
    m, d = xg.shape
    n = w.shape[2]
    tn = PROJ_TN
    rope = rope_tabs is not None
    caches = cache_shapes is not None
    in_specs = [pl.BlockSpec((tm, d), lambda i, j: (i, 0)),
                pl.BlockSpec((None, d, tn), lambda i, j: (layer, 0, j)),
                pl.BlockSpec((tm, SSQ_LANES), lambda i, j: (i, 0)),
                pl.BlockSpec((None, 1, tn), lambda i, j: (mod.index(i, tm), 0, j))]
    args = [xg, w, ssq, shw]
    if rope:
        tiles_per_seq = seq // tm
        tab_spec = pl.BlockSpec((tm, HEAD_DIM), lambda i, j: (i % tiles_per_seq, 0))
        in_specs += [tab_spec] * 3
        args += list(rope_tabs)
    n_alias = 0
    aliases = {}
    if prev_caches is not None:
        n_alias = len(prev_caches)
        for t, arr in enumerate(prev_caches):
            aliases[len(args)] = 2 + t
            in_specs.append(pl.BlockSpec(memory_space=pl.ANY))
            args.append(arr)
    out_specs = [pl.BlockSpec((tm, FNET_W), lambda i, j: (i, 0)), pl.BlockSpec((tm, tn), lambda i, j: (i, j))]
    out_shape = [jax.ShapeDtypeStruct((m, FNET_W), F32), jax.ShapeDtypeStruct((m, n), BF16)]
    est = 2 * _nbytes((tm, d), BF16) + 2 * _nbytes((d, tn), BF16) + 2 * _nbytes((tm, FNET_W), F32)
    est += 2 * _nbytes((tm, tn), BF16) + 2 * _nbytes((tm, tn), F32) + 6 * _nbytes((tm, HEAD_DIM), F32)
    if caches:
        seqs = tm // seq
        for shp in cache_shapes:
            out_specs.append(pl.BlockSpec((seqs, None, seq, shp[3]), lambda i, j: (i, layer, 0, 0)))
            out_shape.append(jax.ShapeDtypeStruct(shp, F32))
            est += 2 * _nbytes((tm, shp[3]), F32)
    return pl.pallas_call(
        functools.partial(_proj_kernel, rope=rope, caches=caches, n_alias=n_alias),
        grid=(m // tm, n // tn),
        in_specs=in_specs,
        out_specs=out_specs,
        out_shape=out_shape,
        input_output_aliases=aliases,
        compiler_params=_cparams(("arbitrary", "arbitrary"), est),
        name="in_proj",
    )(*args)


def _sigmoid(z):
    return 1.0 / (1.0 + jnp.exp(-z))


def _merge_kernel(xg_ref, ssq_ref, ya_ref, yb_ref, yc_ref, wga_ref, wgb_ref, wgc_ref, bga_ref, bgb_ref, bgc_ref,
                  sga_ref, sgb_ref, sgc_ref, wa_ref, wb_ref, wc_ref, o_ref):
    xg = xg_ref[...]
    r = _row_scale(ssq_ref, xg.shape[1])

    def branch(wg_ref, bg_ref, sg_ref, y_ref, w_ref):
        pre = r * jnp.dot(xg, wg_ref[...], preferred_element_type=F32) + (sg_ref[...] + bg_ref[...])
        return _sigmoid(pre) * jnp.dot(y_ref[...], w_ref[...], preferred_element_type=F32)

    m = branch(wga_ref, bga_ref, sga_ref, ya_ref, wa_ref)
    m = m + branch(wgb_ref, bgb_ref, sgb_ref, yb_ref, wb_ref)
    m = m + branch(wgc_ref, bgc_ref, sgc_ref, yc_ref, wc_ref)
    o_ref[...] = m.astype(o_ref.dtype)


def merge(xg, ssq, ya, yb, yc, w_gate, b_gate, shw_gate, w_a, w_b, w_c, layer, mod, *, tm=512, tn=512):
    m, d = xg.shape
    nj = d // tn
    depth = w_gate.shape[0]
    b_gate = b_gate.reshape(depth, 1, 3 * d)
    row = lambda i, j: (i, 0)

    def stack_spec(rows, part):
        return pl.BlockSpec((None, rows, tn), lambda i, j: (layer, 0, part * nj + j))

    def shw_spec(part):
        return pl.BlockSpec((None, 1, tn), lambda i, j: (mod.index(i, tm), 0, part * nj + j))

    in_specs = [
        pl.BlockSpec((tm, d), row),
        pl.BlockSpec((tm, SSQ_LANES), row),
        pl.BlockSpec((tm, ya.shape[1]), row),
        pl.BlockSpec((tm, yb.shape[1]), row),
        pl.BlockSpec((tm, yc.shape[1]), row),
        stack_spec(d, 0), stack_spec(d, 1), stack_spec(d, 2),
        stack_spec(1, 0), stack_spec(1, 1), stack_spec(1, 2),
        shw_spec(0), shw_spec(1), shw_spec(2),
        stack_spec(w_a.shape[1], 0), stack_spec(w_b.shape[1], 0), stack_spec(w_c.shape[1], 0),
    ]
    est = 2 * 2 * _nbytes((tm, d), BF16) + 2 * 4 * _nbytes((d, tn), BF16) + 8 * _nbytes((tm, tn), F32)
    return pl.pallas_call(
        _merge_kernel,
        grid=(m // tm, nj),
        in_specs=in_specs,
        out_specs=pl.BlockSpec((tm, tn), lambda i, j: (i, j)),
        out_shape=jax.ShapeDtypeStruct((m, d), BF16),
        compiler_params=_cparams(("arbitrary", "arbitrary"), est),
        name="merge",
    )(xg, ssq, ya, yb, yc, w_gate, w_gate, w_gate, b_gate, b_gate, b_gate, shw_gate, shw_gate, shw_gate,
      w_a, w_b, w_c)


def _split_hi_lo(x):
    hi = x.astype(BF16)
    lo = (x - hi.astype(F32)).astype(BF16)
    return hi, lo


def _dot3(xh, xl, wh, wl):
    acc = jnp.dot(xh, wh, preferred_element_type=F32)
    acc += jnp.dot(xl, wh, preferred_element_type=F32)
    acc += jnp.dot(xh, wl, preferred_element_type=F32)
    return acc


def _dot_nt(a, b):
    return lax.dot_general(a, b, (((1,), (1,)), ((), ())), preferred_element_type=F32)


def _row_max(scores, extra=None):
    m = scores[0].max(axis=-1, keepdims=True)
    for s in scores[1:]:
        m = jnp.maximum(m, s.max(axis=-1, keepdims=True))
    if extra is not None:
        m = jnp.maximum(m, extra)
    return m


def _sink_column(sink_ref, hk, rows_per_head):
    rows = WIN_GROUP * rows_per_head
    group = lax.broadcasted_iota(jnp.int32, (rows, 1), 0) // rows_per_head
    sink = jnp.zeros((rows, 1), F32)
    for g in range(WIN_GROUP):
        sink = jnp.where(group == g, sink_ref[hk * WIN_GROUP + g], sink)
    return sink


def _stacked_q(q_refs, hk):
    heads_per_ref = q_refs[0].shape[1] // HEAD_DIM
    parts = []
    for g in range(WIN_GROUP):
        h = hk * WIN_GROUP + g
        ref = q_refs[h // heads_per_ref]
        lo = (h % heads_per_ref) * HEAD_DIM
        parts.append(ref[:, lo:lo + HEAD_DIM])
    return jnp.concatenate(parts, axis=0)


def _ctx_mixer_kernel(a_ref, bq0_ref, bq1_ref, bk_ref, bv_ref, cq_ref, ck_ref, cv_ref, sink_ref,
                      ch_h_ref, ch_l_ref, pc_h_ref, pc_l_ref, ps_h_ref, ps_l_ref,
                      ya_ref, yb_ref, yc_ref, *, seq):
    gw = FNET_GROUP_W
    ch_h = ch_h_ref[...]
    ch_l = ch_l_ref[...]
    pc_h, pc_l, ps_h, ps_l = pc_h_ref[...], pc_l_ref[...], ps_h_ref[...], ps_l_ref[...]
    norm = 1.0 / math.sqrt(seq * gw)
    for g in range(FNET_W // gw):
        a_h, a_l = _split_hi_lo(a_ref[:, g * gw:(g + 1) * gw])
        t = _dot3(a_h, a_l, ch_h, ch_l)
        tc_h, tc_l = _split_hi_lo(t[:, :gw])
        ts_h, ts_l = _split_hi_lo(t[:, gw:])
        y = _dot3(pc_h, pc_l, tc_h, tc_l) - _dot3(ps_h, ps_l, ts_h, ts_l)
        ya_ref[:, g * gw:(g + 1) * gw] = (y * norm).astype(ya_ref.dtype)

    hd = HEAD_DIM
    for hk in range(WIN_KV_HEADS):
        q = _stacked_q((bq0_ref, bq1_ref), hk)
        s = _dot_nt(q, bk_ref[:, hk * hd:(hk + 1) * hd])
        sink = _sink_column(sink_ref, hk, seq)
        m = _row_max([s], sink)
        e = jnp.exp(s - m)
        den = e.sum(axis=-1, keepdims=True) + jnp.exp(sink - m)
        o = jnp.dot(e.astype(BF16), bv_ref[:, hk * hd:(hk + 1) * hd], preferred_element_type=F32) / den
        for g in range(WIN_GROUP):
            h = hk * WIN_GROUP + g
            yb_ref[:, h * hd:(h + 1) * hd] = o[g * seq:(g + 1) * seq].astype(yb_ref.dtype)

    for h in range(NA_HEADS):
        hs = slice(h * hd, (h + 1) * hd)
        s = _dot_nt(cq_ref[:, hs], ck_ref[:, hs])
        m = s.max(axis=-1, keepdims=True)
        e = jnp.exp(s - m)
        den = e.sum(axis=-1, keepdims=True)
        o = jnp.dot(e.astype(BF16), cv_ref[:, hs], preferred_element_type=F32) / den
        yc_ref[:, hs] = o.astype(yc_ref.dtype)


def ctx_mixers(a32, p16, sink, tabs, *, batch, seq):
    def col(width, offset):
        return pl.BlockSpec((seq, width), lambda b: (b, offset // width))

    const = lambda shape: pl.BlockSpec(shape, lambda b: (0,) * len(shape))
    half_q = WIN_Q_W // 2
    in_specs = [
        col(FNET_W, 0), col(half_q, COL_BQ), col(half_q, COL_BQ + half_q),
        col(WIN_KV_W, COL_BK), col(WIN_KV_W, COL_BV),
        col(NA_W, COL_CQ), col(NA_W, COL_CK), col(NA_W, COL_CV),
        pl.BlockSpec(memory_space=pltpu.SMEM),
        const(tabs["ch_h"].shape), const(tabs["ch_l"].shape),
        const(tabs["pc_h"].shape), const(tabs["pc_l"].shape),
        const(tabs["ps_h"].shape), const(tabs["ps_l"].shape),
    ]
    out_specs = [pl.BlockSpec((seq, FNET_W), lambda b: (b, 0)),
                 pl.BlockSpec((seq, WIN_Q_W), lambda b: (b, 0)),
                 pl.BlockSpec((seq, NA_W), lambda b: (b, 0))]
    m = batch * seq
    out_shape = [jax.ShapeDtypeStruct((m, FNET_W), BF16), jax.ShapeDtypeStruct((m, WIN_Q_W), BF16),
                 jax.ShapeDtypeStruct((m, NA_W), BF16)]
    est = 2 * _nbytes((seq, IN_W), F32) + 2 * _nbytes((seq, D_MODEL), BF16) + (16 << 20)
    return pl.pallas_call(
        functools.partial(_ctx_mixer_kernel, seq=seq),
        grid=(batch,),
        in_specs=in_specs,
        out_specs=out_specs,
        out_shape=out_shape,
        compiler_params=_cparams(("arbitrary",), est),
        name="ctx_mixers",
    )(a32, p16, p16, p16, p16, p16, p16, p16, sink,
      tabs["ch_h"], tabs["ch_l"], tabs["pc_h"], tabs["pc_l"], tabs["ps_h"], tabs["ps_l"])


def _group_dot3(x, tab_h, tab_l):
    gw = FNET_GROUP_W
    outs = []
    for g in range(FNET_W // gw):
        x_h, x_l = _split_hi_lo(x[:, g * gw:(g + 1) * gw])
        outs.append(_dot3(x_h, x_l, tab_h, tab_l))
    return jnp.concatenate(outs, axis=1)


def _shifted_reverse(jrev_ref, x_hi, x_lo):
    out = jnp.dot(jrev_ref[...], x_hi, preferred_element_type=F32)
    if x_lo is not None:
        out += jnp.dot(jrev_ref[...], x_lo, preferred_element_type=F32)
    return out


def _dft_fold_kernel(a_ref, am_ref, nb_ref, mid_ref, jrev_ref, ch_h_ref, ch_l_ref, te_ref, to_ref, aux_ref):
    i = pl.program_id(1)
    gw = FNET_GROUP_W
    cc_h, cc_l = ch_h_ref[:, :gw], ch_l_ref[:, :gw]
    sc_h, sc_l = ch_h_ref[:, gw:], ch_l_ref[:, gw:]
    a = a_ref[...]
    tm = a.shape[0]
    row = lax.broadcasted_iota(jnp.int32, (tm, 1), 0)
    am_h, am_l = _split_hi_lo(am_ref[...])
    first = jnp.where(i > 0, nb_ref[0:1, :], 0.0)
    arev = jnp.where(row == 0, first, _shifted_reverse(jrev_ref, am_h, am_l))
    e = a + arev
    o = jnp.where((row == 0) & (i == 0), 0.0, a - arev)

    te_h, te_l = _split_hi_lo(_group_dot3(e, cc_h, cc_l))
    to_h, to_l = _split_hi_lo(_group_dot3(o, sc_h, sc_l))
    te_ref[0], te_ref[1], te_ref[2] = te_h, te_h, te_l
    to_ref[0], to_ref[1], to_ref[2] = to_h, to_h, to_l

    @pl.when(i == 0)
    def _():
        rows8 = lax.broadcasted_iota(jnp.int32, aux_ref.shape, 0)
        aux_ref[...] = _group_dot3(jnp.where(rows8 == 0, mid_ref[0:1, :], 0.0), cc_h, cc_l)


def dft_fold(a32, tabs, *, batch, seq, tm=DFT_FOLD_TM):
    half = seq // 2
    nt = half // tm
    tiles = seq // tm
    sub = 8
    w = FNET_W
    slab = pl.BlockSpec((3, tm, w), lambda b, i: (0, i, b))
    const = lambda arr: pl.BlockSpec(arr.shape, lambda b, i: (0, 0))
    jrev = tabs["jrev_fold"]
    est = 6 * _nbytes((tm, w), F32) + 4 * 3 * _nbytes((tm, w), BF16) + 2 * _nbytes(jrev.shape, BF16) + (12 << 20)
    return pl.pallas_call(
        _dft_fold_kernel,
        grid=(batch, nt),
        in_specs=[pl.BlockSpec((tm, w), lambda b, i: (b * tiles + i, 0)),
                  pl.BlockSpec((tm, w), lambda b, i: (b * tiles + tiles - 1 - i, 0)),
                  pl.BlockSpec((sub, w), lambda b, i: ((b * seq + (seq - i * tm) % seq) // sub, 0)),
                  pl.BlockSpec((sub, w), lambda b, i: ((b * seq + half) // sub, 0)),
                  const(jrev), const(tabs["ch_h"]), const(tabs["ch_l"])],
        out_specs=[slab, slab, pl.BlockSpec((None, sub, w), lambda b, i: (b, 0, 0))],
        out_shape=[jax.ShapeDtypeStruct((3, half, batch * w), BF16),
                   jax.ShapeDtypeStruct((3, half, batch * w), BF16),
                   jax.ShapeDtypeStruct((batch, sub, w), F32)],
        compiler_params=_cparams(("arbitrary", "arbitrary"), est),
        name="dft_fold",
    )(a32, a32, a32, a32, jrev, tabs["ch_h"], tabs["ch_l"])


def _dft_pos_kernel(c3_ref, s3_ref, c3x_ref, s3x_ref, te_ref, to_ref, aux_ref, jrev_ref, lo_ref, hi_ref,
                    accc_ref, accs_ref, accx_ref, *, nk, norm):
    k = pl.program_id(2)

    def dots():
        te, to = te_ref[...], to_ref[...]
        return (jnp.dot(c3_ref[...], te, preferred_element_type=F32),
                jnp.dot(s3_ref[...], to, preferred_element_type=F32),
                jnp.dot(c3x_ref[...], te, preferred_element_type=F32)
                + jnp.dot(s3x_ref[...], to, preferred_element_type=F32))

    @pl.when(k == 0)
    def _():
        accc_ref[...], accs_ref[...], accx_ref[...] = dots()

    @pl.when((k > 0) & (k < nk - 1))
    def _():
        dc, ds, dx = dots()
        accc_ref[...] += dc
        accs_ref[...] += ds
        accx_ref[...] += dx

    @pl.when(k == nk - 1)
    def _():
        dc, ds, dx = dots()
        tm = dc.shape[0]
        row = lax.broadcasted_iota(jnp.int32, (tm, 1), 0)
        mid_c = aux_ref[0:1, :]
        pc = accc_ref[...] + dc + jnp.where(row % 2 == 0, mid_c, -mid_c)
        ps = accs_ref[...] + ds
        lo_ref[...] = ((pc - ps) * norm).astype(lo_ref.dtype)
        mirrored = _shifted_reverse(jrev_ref, ((pc + ps) * norm).astype(BF16), None)
        edge = ((accx_ref[...] + dx)[0:1, :] + mid_c) * norm
        hi_ref[...] = jnp.where(row == 0, edge, mirrored).astype(hi_ref.dtype)


def dft_pos(te, to, aux, tabs, *, batch, seq, tm=DFT_POS_TM, tk=1024):
    half = seq // 2
    w = FNET_W
    kdim = 3 * half
    nk = kdim // tk
    ni = half // tm
    sub = 8
    norm = 1.0 / math.sqrt(seq * FNET_GROUP_W)
    jrev = tabs["jrev_pos"]
    tab_spec = pl.BlockSpec((tm, tk), lambda i, j, k: (i, k))
    edge_spec = pl.BlockSpec((None, sub, tk), lambda i, j, k: (i, 0, k))
    slab_spec = pl.BlockSpec((tk, w), lambda i, j, k: (k, j))
    est = 4 * _nbytes((tm, tk), BF16) + 4 * _nbytes((tk, w), BF16) + 5 * _nbytes((tm, w), F32)
    est += 4 * _nbytes((tm, w), BF16) + 2 * _nbytes(jrev.shape, BF16) + (2 << 20)
    return pl.pallas_call(
        functools.partial(_dft_pos_kernel, nk=nk, norm=norm),
        grid=(ni, batch, nk),
        in_specs=[tab_spec, tab_spec, edge_spec, edge_spec, slab_spec, slab_spec,
                  pl.BlockSpec((None,) + aux.shape[1:], lambda i, j, k: (j, 0, 0)),
                  pl.BlockSpec(jrev.shape, lambda i, j, k: (0, 0))],
        out_specs=[pl.BlockSpec((None, tm, w), lambda i, j, k: (j, i, 0)),
                   pl.BlockSpec((None, tm, w), lambda i, j, k: (j, ni - 1 - i, 0))],
        out_shape=[jax.ShapeDtypeStruct((batch, half, w), BF16)] * 2,
        scratch_shapes=[pltpu.VMEM((tm, w), F32), pltpu.VMEM((tm, w), F32), pltpu.VMEM((sub, w), F32)],
        compiler_params=_cparams(("arbitrary", "arbitrary", "arbitrary"), est),
        name="dft_pos",
    )(tabs["c3"], tabs["s3"], tabs["c3x"], tabs["s3x"],
      te.reshape(kdim, batch * w), to.reshape(kdim, batch * w), aux, jrev)


def latent_fourier(a32, tabs, *, batch, seq):
    te, to, aux = dft_fold(a32, tabs, batch=batch, seq=seq)
    lo, hi = dft_pos(te, to, aux, tabs, batch=batch, seq=seq)
    return jnp.concatenate([lo, hi], axis=1).reshape(batch * seq, FNET_W)


def _win_attn_kernel(q0_ref, q1_ref, k_ref, v_ref, ck_ref, cv_ref, sink_ref, o_ref, *, seq):
    n = pl.program_id(1)
    blk = WIN
    span = 3 * blk
    hd = HEAD_DIM
    start = pl.multiple_of(jnp.clip((n - 1) * blk, 0, seq - span), blk)
    rows = WIN_GROUP * blk
    ipos = n * blk + lax.broadcasted_iota(jnp.int32, (rows, span), 0) % blk
    jpos = start + lax.broadcasted_iota(jnp.int32, (rows, span), 1)
    valid = jnp.abs(ipos - jpos) <= WIN

    for hk in range(WIN_KV_HEADS):
        hs = slice(hk * hd, (hk + 1) * hd)
        q = _stacked_q((q0_ref, q1_ref), hk)
        s = jnp.where(valid, _dot_nt(q, k_ref[pl.ds(start, span), hs]), NEG_INF)
        sc = _dot_nt(q, ck_ref[:, hs].astype(BF16))
        sink = _sink_column(sink_ref, hk, blk)
        m = _row_max([s, sc], sink)
        e = jnp.exp(s - m)
        ec = jnp.exp(sc - m)
        den = e.sum(axis=-1, keepdims=True) + ec.sum(axis=-1, keepdims=True) + jnp.exp(sink - m)
        o = jnp.dot(e.astype(BF16), v_ref[pl.ds(start, span), hs], preferred_element_type=F32)
        o += jnp.dot(ec.astype(BF16), cv_ref[:, hs].astype(BF16), preferred_element_type=F32)
        o = o / den
        for g in range(WIN_GROUP):
            h = hk * WIN_GROUP + g
            o_ref[:, h * hd:(h + 1) * hd] = o[g * blk:(g + 1) * blk].astype(o_ref.dtype)


def win_attention(p16, cache_k, cache_v, sink, *, batch, seq, layer):
    blk = WIN
    nblk = seq // blk
    past = cache_k.shape[2]
    half_q = WIN_Q_W // 2

    def q_spec(offset):
        return pl.BlockSpec((blk, half_q), lambda b, n: (b * nblk + n, offset // half_q))

    def seq_spec(offset):
        return pl.BlockSpec((seq, WIN_KV_W), lambda b, n: (b, offset // WIN_KV_W))

    cache_spec = pl.BlockSpec((None, None, past, WIN_KV_W), lambda b, n: (b, layer, 0, 0))
    in_specs = [q_spec(COL_BQ), q_spec(COL_BQ + half_q), seq_spec(COL_BK), seq_spec(COL_BV),
                cache_spec, cache_spec, pl.BlockSpec(memory_space=pltpu.SMEM)]
    est = 4 * _nbytes((seq, WIN_KV_W), BF16) + 4 * _nbytes((past, WIN_KV_W), F32) + (16 << 20)
    return pl.pallas_call(
        functools.partial(_win_attn_kernel, seq=seq),
        grid=(batch, nblk),
        in_specs=in_specs,
        out_specs=pl.BlockSpec((blk, WIN_Q_W), lambda b, n: (b * nblk + n, 0)),
        out_shape=jax.ShapeDtypeStruct((batch * seq, WIN_Q_W), BF16),
        compiler_params=_cparams(("arbitrary", "arbitrary"), est),
        name="win_attn",
    )(p16, p16, p16, p16, cache_k, cache_v, sink)


def _na_window_start(rg, rows_n):
    return jnp.clip(rg * NA_QROWS - NA_KH // 2, 0, rows_n - NA_KROWS)


def _na_attn_kernel(q_ref, k_ref, v_ref, ck_ref, cv_ref, bias_ref, o_ref, *, rows_n):
    rg = pl.program_id(1)
    hd = HEAD_DIM
    nkeys = NA_KROWS * GRID_W
    start = pl.multiple_of(_na_window_start(rg, rows_n) * GRID_W, GRID_W)
    for h in range(NA_HEADS):
        hs = slice(h * hd, (h + 1) * hd)
        q = q_ref[:, hs]
        s = _dot_nt(q, k_ref[pl.ds(start, nkeys), hs]) + bias_ref[h]
        sc = _dot_nt(q, ck_ref[:, hs].astype(BF16))
        m = _row_max([s, sc])
        e = jnp.exp(s - m)
        ec = jnp.exp(sc - m)
        den = e.sum(axis=-1, keepdims=True) + ec.sum(axis=-1, keepdims=True)
        o = jnp.dot(e.astype(BF16), v_ref[pl.ds(start, nkeys), hs], preferred_element_type=F32)
        o += jnp.dot(ec.astype(BF16), cv_ref[:, hs].astype(BF16), preferred_element_type=F32)
        o_ref[:, hs] = (o / den).astype(o_ref.dtype)


def na_attention(p16, cache_k, cache_v, bias_tab, *, batch, seq, layer):
    rows_n = seq // GRID_W
    ngroups = rows_n // NA_QROWS
    nq = NA_QROWS * GRID_W
    nkeys = NA_KROWS * GRID_W
    past = cache_k.shape[2]

    def seq_spec(offset):
        return pl.BlockSpec((seq, NA_W), lambda b, rg: (b, offset // NA_W), pipeline_mode=pl.Buffered(1))

    def pattern(b, rg):
        return (layer, (rg > 0).astype(jnp.int32) + (rg == ngroups - 1).astype(jnp.int32), 0, 0, 0)

    cache_spec = pl.BlockSpec((None, None, past, NA_W), lambda b, rg: (b, layer, 0, 0))
    in_specs = [pl.BlockSpec((nq, NA_W), lambda b, rg: (b * ngroups + rg, COL_CQ // NA_W)),
                seq_spec(COL_CK), seq_spec(COL_CV), cache_spec, cache_spec,
                pl.BlockSpec((None, None, NA_HEADS, nq, nkeys), pattern, pipeline_mode=pl.Buffered(1))]
    est = 2 * _nbytes((seq, NA_W), BF16) + _nbytes((NA_HEADS, nq, nkeys), F32)
    est += 4 * _nbytes((past, NA_W), F32) + (12 << 20)
    return pl.pallas_call(
        functools.partial(_na_attn_kernel, rows_n=rows_n),
        grid=(batch, ngroups),
        in_specs=in_specs,
        out_specs=pl.BlockSpec((nq, NA_W), lambda b, rg: (b * ngroups + rg, 0)),
        out_shape=jax.ShapeDtypeStruct((batch * seq, NA_W), BF16),
        compiler_params=_cparams(("arbitrary", "arbitrary"), est),
        name="na_attn",
    )(p16, p16, p16, cache_k, cache_v, bias_tab)


def _dft_cos_sin(n):
    idx = jnp.arange(n, dtype=jnp.int32)
    phase = (idx[:, None] * idx[None, :]) % n
    ang = phase.astype(F32) * (2.0 * math.pi / n)
    return jnp.cos(ang), jnp.sin(ang)


def _hi_lo(x):
    hi = x.astype(BF16)
    return hi, (x - hi.astype(F32)).astype(BF16)


def _fourier_tables(ctx_seq, lat_seq):
    cc, sc = _dft_cos_sin(FNET_GROUP_W)
    ch_h, ch_l = _hi_lo(jnp.concatenate([cc, sc], axis=1))
    cp, sp = _dft_cos_sin(ctx_seq)
    pc_h, pc_l = _hi_lo(cp)
    ps_h, ps_l = _hi_lo(sp)
    half = lat_seq // 2
    idx = jnp.arange(half, dtype=jnp.int32)

    def tables(rows):
        ang = ((rows[:, None] * idx[None, :]) % lat_seq).astype(F32) * (2.0 * math.pi / lat_seq)
        c_h, c_l = _hi_lo(jnp.cos(ang))
        s_h, s_l = _hi_lo(jnp.sin(ang))
        return jnp.concatenate([c_h, c_l, c_h], axis=1), jnp.concatenate([s_h, s_l, s_h], axis=1)

    c3, s3 = tables(idx)
    ni = half // DFT_POS_TM
    c3x, s3x = tables((jnp.arange(ni, dtype=jnp.int32) + 1) * DFT_POS_TM)
    pad = lambda t: jnp.pad(t[:, None, :], ((0, 0), (0, 7), (0, 0)))
    return dict(ch_h=ch_h, ch_l=ch_l, pc_h=pc_h, pc_l=pc_l, ps_h=ps_h, ps_l=ps_l, c3=c3, s3=s3,
                c3x=pad(c3x), s3x=pad(s3x),
                jrev_fold=_shifted_antidiagonal(DFT_FOLD_TM), jrev_pos=_shifted_antidiagonal(DFT_POS_TM))


def _shifted_antidiagonal(n):
    r = jnp.arange(n)
    return ((r[:, None] + r[None, :]) == n).astype(BF16)


def _rope_tables(seq):
    t = jnp.arange(seq)
    n_freq = HEAD_DIM // 4
    inv = ROPE_BASE ** (-jnp.arange(n_freq, dtype=F32) / n_freq)
    ang_r = (t // GRID_W).astype(F32)[:, None] * inv
    ang_c = (t % GRID_W).astype(F32)[:, None] * inv
    zeros = jnp.zeros_like(ang_r)
    cos = jnp.concatenate([jnp.cos(ang_r)] * 2 + [jnp.cos(ang_c)] * 2, axis=1)
    sin_lo = jnp.concatenate([-jnp.sin(ang_r), zeros, -jnp.sin(ang_c), zeros], axis=1)
    sin_hi = jnp.concatenate([zeros, jnp.sin(ang_r), zeros, jnp.sin(ang_c)], axis=1)
    return cos, sin_lo, sin_hi


def _na_bias_tables(rpb, rows_n):
    depth = rpb.shape[0]
    col = jnp.arange(GRID_W)
    cs = jnp.clip(col - NA_KW // 2, 0, GRID_W - NA_KW)
    col_ok = (col[None, :] >= cs[:, None]) & (col[None, :] < cs[:, None] + NA_KW)
    idx_c = jnp.clip(col[None, :] - col[:, None], -(NA_KW - 1), NA_KW - 1) + NA_KW - 1
    bt = jnp.where(col_ok[None, None, None], rpb.astype(F32)[:, :, :, idx_c], NEG_INF)

    ngroups = rows_n // NA_QROWS
    groups = jnp.array([0, 1, ngroups - 1])
    r = groups[:, None] * NA_QROWS + jnp.arange(NA_QROWS)[None, :]
    ws = jnp.clip(groups * NA_QROWS - NA_KH // 2, 0, rows_n - NA_KROWS)
    krow = ws[:, None] + jnp.arange(NA_KROWS)[None, :]
    rs = jnp.clip(r - NA_KH // 2, 0, rows_n - NA_KH)
    row_ok = (krow[:, None, :] >= rs[:, :, None]) & (krow[:, None, :] < rs[:, :, None] + NA_KH)
    rho = jnp.clip(krow[:, None, :] - r[:, :, None] + NA_KH - 1, 0, 2 * NA_KH - 2)
    tab = jnp.where(row_ok[None, None, :, :, :, None, None], bt[:, :, rho], NEG_INF)
    tab = jnp.transpose(tab, (0, 2, 1, 3, 5, 4, 6))
    return tab.reshape(depth, 3, NA_HEADS, NA_QROWS * GRID_W, NA_KROWS * GRID_W)


def kernel(x_prompt, x_sample, cache_win_k, cache_win_v, cache_na_k, cache_na_v, c, c_ctx, w_mod, b_mod,
           norm1_g, w_in, win_sink, na_rpb, w_a, w_b, w_c, w_gate, b_gate, w_out, norm2_g, w1, w2, final_g):
    batch, seq, d = x_prompt.shape
    dec_batch, dec_seq, _ = x_sample.shape
    past = cache_win_k.shape[2]
    m_ctx, m_lat = batch * seq, dec_batch * dec_seq

    w_in_b = w_in.astype(BF16)
    w_a_b, w_b_b, w_c_b = w_a.astype(BF16), w_b.astype(BF16), w_c.astype(BF16)
    w_gate_b, w_out_b = w_gate.astype(BF16), w_out.astype(BF16)
    w1_b, w2_b = w1.astype(BF16), w2.astype(BF16)

    cvec = jnp.concatenate([c_ctx[None], c, jnp.zeros((N_MOD - 1 - dec_batch, d), F32)], axis=0)
    mods = adaln_all(cvec, w_mod, b_mod).reshape(DEPTH, N_MOD, 6, 1, d)
    shifts = jnp.stack([mods[:, :, 0, 0], mods[:, :, 3, 0]], axis=2)
    shifts = jnp.pad(shifts, ((0, 0), (0, MOD_PAD_ROWS - N_MOD), (0, 0), (0, 0))).astype(BF16)

    ftabs = _fourier_tables(seq, dec_seq)
    rope_tabs = _rope_tables(dec_seq)
    na_bias = _na_bias_tables(na_rpb, dec_seq // GRID_W)
    cwk, cwv = (t.reshape(dec_batch, DEPTH, past, WIN_KV_W) for t in (cache_win_k, cache_win_v))
    cnk, cnv = (t.reshape(dec_batch, DEPTH, past, NA_W) for t in (cache_na_k, cache_na_v))
    cache_shapes = [(batch, DEPTH, seq, WIN_KV_W)] * 2 + [(batch, DEPTH, seq, NA_W)] * 2

    group_mods = (ModRows(0, m_ctx), ModRows(1, dec_seq))
    xs_by_group = [x_prompt.reshape(m_ctx, d), x_sample.reshape(m_lat, d)]
    pending = [prep_norm(x, norm1_g[0], mods[0, :, 1], mod) for x, mod in zip(xs_by_group, group_mods)]
    new_caches = None
    for l in range(DEPTH):
        sh1_w_in = mod_bias(shifts[l, :, 0], w_in_b, l)
        sh1_w_gate = mod_bias(shifts[l, :, 0], w_gate_b, l)
        sh2_w1 = mod_bias(shifts[l, :, 1], w1_b, l)
        gate1, scale2, gate2 = mods[l, :, 2], mods[l, :, 4], mods[l, :, 5]
        for gi, mod in enumerate(group_mods):
            x = xs_by_group[gi]
            xg, ssq = pending[gi]
            if gi == 0:
                a32, p16, *new_caches = in_projection(xg, ssq, w_in_b, sh1_w_in, l, mod, seq=seq,
                                                      cache_shapes=cache_shapes, prev_caches=new_caches, tm=512)
                ya, yb, yc = ctx_mixers(a32, p16, win_sink[l], ftabs, batch=batch, seq=seq)
            else:
                a32, p16 = in_projection(xg, ssq, w_in_b, sh1_w_in, l, mod, rope_tabs, seq=dec_seq)
                ya = latent_fourier(a32, ftabs, batch=dec_batch, seq=dec_seq)
                yb = win_attention(p16, cwk, cwv, win_sink[l], batch=dec_batch, seq=dec_seq, layer=l)
                yc = na_attention(p16, cnk, cnv, na_bias, batch=dec_batch, seq=dec_seq, layer=l)
            mrg = merge(xg, ssq, ya, yb, yc, w_gate_b, b_gate, sh1_w_gate, w_a_b, w_b_b, w_c_b, l, mod)
            x, xg2, ssq2 = matmul_resid(mrg, w_out_b, l, x, gate1, mod, next_norm=(norm2_g[l], scale2), tn=512)
            u = mlp_up(xg2, ssq2, w1_b, sh2_w1, l, mod)
            if l + 1 < DEPTH:
                x, xg, ssq = matmul_resid(u, w2_b, l, x, gate2, mod, tk=2048,
                                          next_norm=(norm1_g[l + 1], mods[l + 1, :, 1]))
                pending[gi] = (xg, ssq)
            else:
                x = matmul_resid(u, w2_b, l, x, gate2, mod, tk=2048)
            xs_by_group[gi] = x

    y_prompt = final_norm(xs_by_group[0], final_g).reshape(batch, seq, d)
    y_sample = final_norm(xs_by_group[1], final_g).reshape(dec_batch, dec_seq, d)
    wk, wv, nk, nv = new_caches
    kv_shape = lambda heads: (batch, DEPTH, seq, heads, HEAD_DIM)
    return (y_prompt, y_sample, wk.reshape(kv_shape(WIN_KV_HEADS)), wv.reshape(kv_shape(WIN_KV_HEADS)),
            nk.reshape(kv_shape(NA_HEADS)), nv.reshape(kv_shape(NA_HEADS)))
```

```python
import functools
import math

import jax
import jax.numpy as jnp
from jax import lax
from jax.experimental import pallas as pl
from jax.experimental.pallas import tpu as pltpu

F32 = jnp.float32
BF16 = jnp.bfloat16

D_MODEL = 4096
DEPTH = 4
GRID_W = 64
HEAD_DIM = 128
FNET_GROUP_W = 256
FNET_W = 1024
WIN_Q_HEADS = 16
WIN_KV_HEADS = 4
WIN_GROUP = WIN_Q_HEADS // WIN_KV_HEADS
WIN = 128
WIN_Q_W = WIN_Q_HEADS * HEAD_DIM
WIN_KV_W = WIN_KV_HEADS * HEAD_DIM
NA_HEADS = 8
NA_KH = 8
NA_KW = 16
NA_W = NA_HEADS * HEAD_DIM
MLP_HIDDEN = 4 * D_MODEL
IN_W = 7168
ROPE_BASE = 10000.0
EPS = 1e-6
NEG_INF = -1e30
ATTN_SCALE = HEAD_DIM ** -0.5
N_MOD = 8
MOD_PAD_ROWS = 16

COL_A, COL_BQ, COL_BK, COL_BV, COL_CQ, COL_CK, COL_CV = 0, 1024, 3072, 3584, 4096, 5120, 6144
PROJ_TN = 1024

NA_QROWS = 4
NA_KROWS = NA_QROWS + NA_KH

DFT_FOLD_TM = 512
DFT_POS_TM = 1024

V7X_SCOPED_VMEM_BYTES = 60000 * 1024
VMEM_MARGIN_BYTES = 6 << 20


def _cparams(semantics, est_bytes):
    limit = min(V7X_SCOPED_VMEM_BYTES, int(est_bytes) + VMEM_MARGIN_BYTES)
    return pltpu.CompilerParams(dimension_semantics=semantics, vmem_limit_bytes=limit)


def _nbytes(shape, dtype):
    return math.prod(shape) * jnp.dtype(dtype).itemsize


def _adaln_kernel(c_ref, w_ref, b_ref, o_ref):
    c = c_ref[...]
    s = (c * (1.0 / (1.0 + jnp.exp(-c)))).astype(BF16)
    o_ref[...] = jnp.dot(s, w_ref[...].astype(BF16), preferred_element_type=F32) + b_ref[...]


def adaln_all(cvec, w_mod, b_mod, tn=512):
    depth, d, n = w_mod.shape
    est = 2 * _nbytes((d, tn), F32) + _nbytes((d, tn), BF16) + 4 * _nbytes((N_MOD, d), F32)
    return pl.pallas_call(
        _adaln_kernel,
        grid=(depth, n // tn),
        in_specs=[
            pl.BlockSpec((N_MOD, d), lambda l, j: (0, 0)),
            pl.BlockSpec((None, d, tn), lambda l, j: (l, 0, j)),
            pl.BlockSpec((None, 1, tn), lambda l, j: (l, 0, j)),
        ],
        out_specs=pl.BlockSpec((None, N_MOD, tn), lambda l, j: (l, 0, j)),
        out_shape=jax.ShapeDtypeStruct((depth, N_MOD, n), F32),
        compiler_params=_cparams(("arbitrary", "arbitrary"), est),
        name="adaln",
    )(cvec, w_mod, b_mod.reshape(depth, 1, n))


SSQ_LANES = 128


class ModRows:
    def __init__(self, offset, rows_per_mod):
        self.offset = offset
        self.rows_per_mod = rows_per_mod

    def index(self, i, tm):
        return self.offset + (i * tm) // self.rows_per_mod


def _row_scale(ssq_ref, d):
    return lax.rsqrt(ssq_ref[:, 0:1] * (1.0 / d) + EPS)


def _row_ssq(x, shape):
    return jnp.broadcast_to(jnp.sum(x * x, axis=-1, keepdims=True), shape)


def _prep_kernel(x_ref, g_ref, sc_ref, xg_ref, ssq_ref):
    x = x_ref[...]
    xg_ref[...] = (x * (g_ref[...] * (1.0 + sc_ref[...]))).astype(xg_ref.dtype)
    ssq_ref[...] = _row_ssq(x, ssq_ref.shape)


def prep_norm(x, g, sc, mod, *, tr=256):
    m, d = x.shape
    est = 2 * _nbytes((tr, d), F32) + 2 * _nbytes((tr, d), BF16) + 3 * _nbytes((tr, d), F32)
    return pl.pallas_call(
        _prep_kernel,
        grid=(m // tr,),
        in_specs=[pl.BlockSpec((tr, d), lambda i: (i, 0)),
                  pl.BlockSpec((1, d), lambda i: (0, 0)),
                  pl.BlockSpec((None, 1, d), lambda i: (mod.index(i, tr), 0, 0))],
        out_specs=[pl.BlockSpec((tr, d), lambda i: (i, 0)), pl.BlockSpec((tr, SSQ_LANES), lambda i: (i, 0))],
        out_shape=[jax.ShapeDtypeStruct((m, d), BF16), jax.ShapeDtypeStruct((m, SSQ_LANES), F32)],
        compiler_params=_cparams(("arbitrary",), est),
        name="prep_norm",
    )(x, g.reshape(1, d), sc)


def _cast_bias_kernel(w_ref, sh_ref, wb_ref, o_ref):
    wb = w_ref[...].astype(BF16)
    wb_ref[...] = wb
    o_ref[...] = jnp.dot(sh_ref[...], wb, preferred_element_type=F32)


def cast_bias(w, sh, *, tn=512):
    depth, d, n = w.shape
    rows = sh.shape[1]
    est = 2 * _nbytes((d, tn), F32) + 3 * _nbytes((d, tn), BF16) + 4 * _nbytes((rows, d), BF16)
    wb, out = pl.pallas_call(
        _cast_bias_kernel,
        grid=(depth, n // tn),
        in_specs=[pl.BlockSpec((None, d, tn), lambda l, j: (l, 0, j)),
                  pl.BlockSpec((None, rows, d), lambda l, j: (l, 0, 0))],
        out_specs=[pl.BlockSpec((None, d, tn), lambda l, j: (l, 0, j)),
                   pl.BlockSpec((None, rows, tn), lambda l, j: (l, 0, j))],
        out_shape=[jax.ShapeDtypeStruct((depth, d, n), BF16), jax.ShapeDtypeStruct((depth, rows, n), F32)],
        compiler_params=_cparams(("arbitrary", "arbitrary"), est),
        name="cast_bias",
    )(w, sh)
    return wb, out.reshape(depth, rows, 1, n)


def _final_norm_kernel(x_ref, g_ref, o_ref):
    x = x_ref[...]
    y = x * lax.rsqrt(jnp.mean(x * x, axis=-1, keepdims=True) + EPS)
    o_ref[...] = (y * g_ref[...]).astype(o_ref.dtype)


def final_norm(x, g, *, tr=256):
    m, d = x.shape
    est = 4 * _nbytes((tr, d), F32) + 3 * _nbytes((tr, d), F32)
    return pl.pallas_call(
        _final_norm_kernel,
        grid=(m // tr,),
        in_specs=[pl.BlockSpec((tr, d), lambda i: (i, 0)), pl.BlockSpec((1, d), lambda i: (0, 0))],
        out_specs=pl.BlockSpec((tr, d), lambda i: (i, 0)),
        out_shape=jax.ShapeDtypeStruct((m, d), F32),
        compiler_params=_cparams(("arbitrary",), est),
        name="final_norm",
    )(x, g.reshape(1, d))


def _mod_bias_kernel(sh_ref, w_ref, o_ref):
    o_ref[...] = jnp.dot(sh_ref[...], w_ref[...], preferred_element_type=F32)


def mod_bias(sh, w, layer, *, tn=1024):
    rows, d = sh.shape
    n = w.shape[-1]
    est = 2 * _nbytes((d, tn), BF16) + 4 * _nbytes((rows, d), BF16) + 4 * _nbytes((rows, tn), F32)
    out = pl.pallas_call(
        _mod_bias_kernel,
        grid=(n // tn,),
        in_specs=[pl.BlockSpec((rows, d), lambda j: (0, 0)),
                  _weight_spec(w, layer, d, tn, lambda j: (0, j))],
        out_specs=pl.BlockSpec((rows, tn), lambda j: (0, j)),
        out_shape=jax.ShapeDtypeStruct((rows, n), F32),
        compiler_params=_cparams(("arbitrary",), est),
        name="mod_bias",
    )(sh, w)
    return out.reshape(rows, 1, n)


def _weight_spec(w, layer, rows, tn, index):
    if layer is None:
        return pl.BlockSpec((rows, tn), index)
    return pl.BlockSpec((None, rows, tn), lambda *ids: (layer,) + tuple(index(*ids)))


def _mlp_up_kernel(*refs, side):
    xg_ref, w_ref, ssq_ref, shw_ref = refs[:4]
    o_ref = refs[5] if side else refs[4]
    acc = jnp.dot(xg_ref[...], w_ref[...], preferred_element_type=F32)
    a = jnp.maximum(_row_scale(ssq_ref, xg_ref.shape[1]) * acc + shw_ref[...], 0.0)
    o_ref[...] = (a * a).astype(o_ref.dtype)
    if side:
        src_ref, dst_ref = refs[4], refs[6]
        dst_ref[...] = src_ref[...].astype(dst_ref.dtype)


def mlp_up(xg, ssq, w, shw, layer, mod, *, side_cast=None, tm=1024, tn=1024):
    m, d = xg.shape
    n = w.shape[-1]
    nj = n // tn
    steps = (m // tm) * nj
    in_specs = [pl.BlockSpec((tm, d), lambda i, j: (i, 0)),
                _weight_spec(w, layer, d, tn, lambda i, j: (0, j)),
                pl.BlockSpec((tm, SSQ_LANES), lambda i, j: (i, 0)),
                pl.BlockSpec((None, 1, tn), lambda i, j: (mod.index(i, tm), 0, j))]
    args = [xg, w, ssq, shw]
    out_specs = [pl.BlockSpec((tm, tn), lambda i, j: (i, j))]
    out_shape = [jax.ShapeDtypeStruct((m, n), BF16)]
    est = 2 * _nbytes((tm, d), BF16) + 2 * _nbytes((d, tn), BF16) + 2 * _nbytes((tm, tn), BF16)
    est += 2 * _nbytes((tm, tn), F32) + 2 * _nbytes((tm, SSQ_LANES), F32)
    if side_cast is not None:
        stack, src_layer = side_cast
        depth, src_k, src_n = stack.shape
        rows = src_k // steps
        in_specs.append(pl.BlockSpec((None, None, rows, src_n), lambda i, j: (src_layer, i * nj + j, 0, 0)))
        args.append(stack.reshape(depth, steps, rows, src_n))
        out_specs.append(pl.BlockSpec((None, rows, src_n), lambda i, j: (i * nj + j, 0, 0)))
        out_shape.append(jax.ShapeDtypeStruct((steps, rows, src_n), BF16))
        est += 3 * _nbytes((rows, src_n), F32) + 2 * _nbytes((rows, src_n), BF16)
    outs = pl.pallas_call(
        functools.partial(_mlp_up_kernel, side=side_cast is not None),
        grid=(m // tm, nj),
        in_specs=in_specs,
        out_specs=out_specs,
        out_shape=out_shape,
        compiler_params=_cparams(("arbitrary", "arbitrary"), est),
        name="mlp_up",
    )(*args)
    if side_cast is None:
        return outs[0]
    return outs[0], outs[1].reshape(src_k, src_n)


def _mm_resid_kernel(*refs, nk, emit):
    x_ref, w_ref, r_ref, gate_ref = refs[:4]
    pos = 4
    if emit:
        gn_ref, scn_ref = refs[pos:pos + 2]
        pos += 2
    o_ref = refs[pos]
    pos += 1
    if emit:
        xg_ref, ssq_ref = refs[pos:pos + 2]
        pos += 2
    j = pl.program_id(1)

    def part():
        return jnp.dot(x_ref[...], w_ref[...], preferred_element_type=F32)

    def finish(acc):
        xn = r_ref[...] + gate_ref[...] * acc
        o_ref[...] = xn
        if emit:
            xg_ref[...] = (xn * (gn_ref[...] * (1.0 + scn_ref[...]))).astype(xg_ref.dtype)
            ssq = _row_ssq(xn, ssq_ref.shape)

            @pl.when(j == 0)
            def _():
                ssq_ref[...] = ssq

            @pl.when(j > 0)
            def _():
                ssq_ref[...] += ssq

    if nk == 1:
        finish(part())
        return
    acc_ref = refs[pos]
    k = pl.program_id(2)

    @pl.when(k == 0)
    def _():
        acc_ref[...] = part()

    @pl.when((k > 0) & (k < nk - 1))
    def _():
        acc_ref[...] += part()

    @pl.when(k == nk - 1)
    def _():
        finish(acc_ref[...] + part())


def matmul_resid(x, w, layer, resid, gate, mod, *, next_norm=None, tm=1024, tn=1024, tk=None):
    m, kdim = x.shape
    n = w.shape[-1]
    tk = kdim if tk is None else tk
    nk = kdim // tk
    emit = next_norm is not None
    mod_spec = pl.BlockSpec((None, 1, tn), lambda i, j, k: (mod.index(i, tm), 0, j))
    tile_spec = pl.BlockSpec((tm, tn), lambda i, j, k: (i, j))
    in_specs = [pl.BlockSpec((tm, tk), lambda i, j, k: (i, k)),
                _weight_spec(w, layer, tk, tn, lambda i, j, k: (k, j)),
                tile_spec, mod_spec]
    args = [x, w, resid, gate]
    out_specs = [tile_spec]
    out_shape = [jax.ShapeDtypeStruct((m, n), F32)]
    est = 2 * _nbytes((tm, tk), BF16) + 2 * _nbytes((tk, tn), BF16) + 6 * _nbytes((tm, tn), F32)
    if emit:
        g, sc = next_norm
        in_specs += [pl.BlockSpec((1, tn), lambda i, j, k: (0, j)), mod_spec]
        args += [g.reshape(1, n), sc]
        out_specs += [tile_spec, pl.BlockSpec((tm, SSQ_LANES), lambda i, j, k: (i, 0))]
        out_shape += [jax.ShapeDtypeStruct((m, n), BF16), jax.ShapeDtypeStruct((m, SSQ_LANES), F32)]
        est += 2 * _nbytes((tm, tn), BF16) + 2 * _nbytes((tm, SSQ_LANES), F32)
    if nk > 1:
        est += _nbytes((tm, tn), F32)
    outs = pl.pallas_call(
        functools.partial(_mm_resid_kernel, nk=nk, emit=emit),
        grid=(m // tm, n // tn, nk),
        in_specs=in_specs,
        out_specs=out_specs,
        out_shape=out_shape,
        scratch_shapes=[pltpu.VMEM((tm, tn), F32)] if nk > 1 else [],
        compiler_params=_cparams(("arbitrary", "arbitrary", "arbitrary"), est),
        name="mm_resid",
    )(*args)
    return outs if emit else outs[0]


def _rope(x, cos, sin_lo, sin_hi):
    quarter = HEAD_DIM // 4
    return (x * cos + pltpu.roll(x, HEAD_DIM - quarter, 1) * sin_lo + pltpu.roll(x, quarter, 1) * sin_hi)


def _proj_tile_plan(rope, caches):
    heads_per_tile = PROJ_TN // HEAD_DIM
    plan = {}
    for j in range(IN_W // PROJ_TN):
        ops = []
        for hh in range(heads_per_tile):
            col = j * PROJ_TN + hh * HEAD_DIM
            is_bq = COL_BQ <= col < COL_BK
            is_bk = COL_BK <= col < COL_BV
            is_cq = COL_CQ <= col < COL_CK
            ops.append((rope and (is_bq or is_bk), is_bq or is_cq))
        col0 = j * PROJ_TN
        f32_use = None
        if col0 == COL_A:
            f32_use = "a"
        elif caches and col0 == COL_BK:
            f32_use = "win_kv"
        elif caches and col0 == COL_CK:
            f32_use = "na_k"
        elif caches and col0 == COL_CV:
            f32_use = "na_v"
        plan.setdefault((tuple(ops), f32_use), []).append(j)
    return plan


def _proj_kernel(*refs, rope, caches, n_alias):
    refs = list(refs)
    xg_ref, w_ref, ssq_ref, shw_ref = refs[:4]
    pos = 4
    if rope:
        cos_ref, slo_ref, shi_ref = refs[pos:pos + 3]
        pos += 3
    pos += n_alias
    a32_ref, o16_ref = refs[pos:pos + 2]
    if caches:
        wk_ref, wv_ref, nk_ref, nv_ref = refs[pos + 2:pos + 6]
    j = pl.program_id(1)

    def run(ops, f32_use):
        acc = jnp.dot(xg_ref[...], w_ref[...], preferred_element_type=F32)
        acc = _row_scale(ssq_ref, xg_ref.shape[1]) * acc + shw_ref[...]
        if f32_use == "a":
            a32_ref[...] = acc
        elif f32_use == "win_kv":
            wk_ref[...] = acc[:, :WIN_KV_W].reshape(wk_ref.shape)
            wv_ref[...] = acc[:, WIN_KV_W:].reshape(wv_ref.shape)
        elif f32_use == "na_k":
            nk_ref[...] = acc.reshape(nk_ref.shape)
        elif f32_use == "na_v":
            nv_ref[...] = acc.reshape(nv_ref.shape)
        for hh, (do_rope, do_scale) in enumerate(ops):
            cols = slice(hh * HEAD_DIM, (hh + 1) * HEAD_DIM)
            x = acc[:, cols]
            if do_rope:
                x = _rope(x, cos_ref[...], slo_ref[...], shi_ref[...])
            if do_scale:
                x = x * ATTN_SCALE
            o16_ref[:, cols] = x.astype(BF16)

    for (ops, f32_use), tiles in _proj_tile_plan(rope, caches).items():
        cond = j == tiles[0]
        for t in tiles[1:]:
            cond = cond | (j == t)
        pl.when(cond)(functools.partial(run, ops, f32_use))


def in_projection(xg, ssq, w, shw, layer, mod, rope_tabs=None, *, seq, cache_shapes=None, prev_caches=None,
                  tm=1024):
    m, d = xg.shape
    n = w.shape[2]
    tn = PROJ_TN
    rope = rope_tabs is not None
    caches = cache_shapes is not None
    in_specs = [pl.BlockSpec((tm, d), lambda i, j: (i, 0)),
                pl.BlockSpec((None, d, tn), lambda i, j: (layer, 0, j)),
                pl.BlockSpec((tm, SSQ_LANES), lambda i, j: (i, 0)),
                pl.BlockSpec((None, 1, tn), lambda i, j: (mod.index(i, tm), 0, j))]
    args = [xg, w, ssq, shw]
    if rope:
        tiles_per_seq = seq // tm
        tab_spec = pl.BlockSpec((tm, HEAD_DIM), lambda i, j: (i % tiles_per_seq, 0))
        in_specs += [tab_spec] * 3
        args += list(rope_tabs)
    n_alias = 0
    aliases = {}
    if prev_caches is not None:
        n_alias = len(prev_caches)
        for t, arr in enumerate(prev_caches):
            aliases[len(args)] = 2 + t
            in_specs.append(pl.BlockSpec(memory_space=pl.ANY))
            args.append(arr)
    out_specs = [pl.BlockSpec((tm, FNET_W), lambda i, j: (i, 0)), pl.BlockSpec((tm, tn), lambda i, j: (i, j))]
    out_shape = [jax.ShapeDtypeStruct((m, FNET_W), F32), jax.ShapeDtypeStruct((m, n), BF16)]
    est = 2 * _nbytes((tm, d), BF16) + 2 * _nbytes((d, tn), BF16) + 2 * _nbytes((tm, FNET_W), F32)
    est += 2 * _nbytes((tm, tn), BF16) + 2 * _nbytes((tm, tn), F32) + 6 * _nbytes((tm, HEAD_DIM), F32)
    if caches:
        seqs = tm // seq
        for shp in cache_shapes:
            out_specs.append(pl.BlockSpec((seqs, None, seq, shp[3]), lambda i, j: (i, layer, 0, 0)))
            out_shape.append(jax.ShapeDtypeStruct(shp, F32))
            est += 2 * _nbytes((tm, shp[3]), F32)
    return pl.pallas_call(
        functools.partial(_proj_kernel, rope=rope, caches=caches, n_alias=n_alias),
        grid=(m // tm, n // tn),
        in_specs=in_specs,
        out_specs=out_specs,
        out_shape=out_shape,
        input_output_aliases=aliases,
        compiler_params=_cparams(("arbitrary", "arbitrary"), est),
        name="in_proj",
    )(*args)


def _sigmoid(z):
    return 1.0 / (1.0 + jnp.exp(-z))


def _merge_kernel(xg_ref, ssq_ref, ya_ref, yb_ref, yc_ref, wga_ref, wgb_ref, wgc_ref, bga_ref, bgb_ref, bgc_ref,
                  sga_ref, sgb_ref, sgc_ref, wa_ref, wb_ref, wc_ref, o_ref):
    xg = xg_ref[...]
    r = _row_scale(ssq_ref, xg.shape[1])

    def branch(wg_ref, bg_ref, sg_ref, y_ref, w_ref):
        pre = r * jnp.dot(xg, wg_ref[...], preferred_element_type=F32) + (sg_ref[...] + bg_ref[...])
        return _sigmoid(pre) * jnp.dot(y_ref[...], w_ref[...], preferred_element_type=F32)

    m = branch(wga_ref, bga_ref, sga_ref, ya_ref, wa_ref)
    m = m + branch(wgb_ref, bgb_ref, sgb_ref, yb_ref, wb_ref)
    m = m + branch(wgc_ref, bgc_ref, sgc_ref, yc_ref, wc_ref)
    o_ref[...] = m.astype(o_ref.dtype)


def merge(xg, ssq, ya, yb, yc, w_gate, b_gate, shw_gate, w_a, w_b, w_c, layer, mod, *, tm=512, tn=512):
    m, d = xg.shape
    nj = d // tn
    depth = w_gate.shape[0]
    b_gate = b_gate.reshape(depth, 1, 3 * d)
    row = lambda i, j: (i, 0)

    def stack_spec(rows, part):
        return pl.BlockSpec((None, rows, tn), lambda i, j: (layer, 0, part * nj + j))

    def shw_spec(part):
        return pl.BlockSpec((None, 1, tn), lambda i, j: (mod.index(i, tm), 0, part * nj + j))

    in_specs = [
        pl.BlockSpec((tm, d), row),
        pl.BlockSpec((tm, SSQ_LANES), row),
        pl.BlockSpec((tm, ya.shape[1]), row),
        pl.BlockSpec((tm, yb.shape[1]), row),
        pl.BlockSpec((tm, yc.shape[1]), row),
        stack_spec(d, 0), stack_spec(d, 1), stack_spec(d, 2),
        stack_spec(1, 0), stack_spec(1, 1), stack_spec(1, 2),
        shw_spec(0), shw_spec(1), shw_spec(2),
        stack_spec(w_a.shape[1], 0), stack_spec(w_b.shape[1], 0), stack_spec(w_c.shape[1], 0),
    ]
    est = 2 * 2 * _nbytes((tm, d), BF16) + 2 * 4 * _nbytes((d, tn), BF16) + 8 * _nbytes((tm, tn), F32)
    return pl.pallas_call(
        _merge_kernel,
        grid=(m // tm, nj),
        in_specs=in_specs,
        out_specs=pl.BlockSpec((tm, tn), lambda i, j: (i, j)),
        out_shape=jax.ShapeDtypeStruct((m, d), BF16),
        compiler_params=_cparams(("arbitrary", "arbitrary"), est),
        name="merge",
    )(xg, ssq, ya, yb, yc, w_gate, w_gate, w_gate, b_gate, b_gate, b_gate, shw_gate, shw_gate, shw_gate,
      w_a, w_b, w_c)


def _split_hi_lo(x):
    hi = x.astype(BF16)
    lo = (x - hi.astype(F32)).astype(BF16)
    return hi, lo


def _dot3(xh, xl, wh, wl):
    acc = jnp.dot(xh, wh, preferred_element_type=F32)
    acc += jnp.dot(xl, wh, preferred_element_type=F32)
    acc += jnp.dot(xh, wl, preferred_element_type=F32)
    return acc


def _dot_nt(a, b):
    return lax.dot_general(a, b, (((1,), (1,)), ((), ())), preferred_element_type=F32)


def _row_max(scores, extra=None):
    m = scores[0].max(axis=-1, keepdims=True)
    for s in scores[1:]:
        m = jnp.maximum(m, s.max(axis=-1, keepdims=True))
    if extra is not None:
        m = jnp.maximum(m, extra)
    return m


def _sink_column(sink_ref, hk, rows_per_head):
    rows = WIN_GROUP * rows_per_head
    group = lax.broadcasted_iota(jnp.int32, (rows, 1), 0) // rows_per_head
    sink = jnp.zeros((rows, 1), F32)
    for g in range(WIN_GROUP):
        sink = jnp.where(group == g, sink_ref[hk * WIN_GROUP + g], sink)
    return sink


def _stacked_q(q_refs, hk):
    heads_per_ref = q_refs[0].shape[1] // HEAD_DIM
    parts = []
    for g in range(WIN_GROUP):
        h = hk * WIN_GROUP + g
        ref = q_refs[h // heads_per_ref]
        lo = (h % heads_per_ref) * HEAD_DIM
        parts.append(ref[:, lo:lo + HEAD_DIM])
    return jnp.concatenate(parts, axis=0)


def _ctx_mixer_kernel(a_ref, bq0_ref, bq1_ref, bk_ref, bv_ref, cq_ref, ck_ref, cv_ref, sink_ref,
                      ch_h_ref, ch_l_ref, pc_h_ref, pc_l_ref, ps_h_ref, ps_l_ref,
                      ya_ref, yb_ref, yc_ref, *, seq):
    gw = FNET_GROUP_W
    ch_h = ch_h_ref[...]
    ch_l = ch_l_ref[...]
    pc_h, pc_l, ps_h, ps_l = pc_h_ref[...], pc_l_ref[...], ps_h_ref[...], ps_l_ref[...]
    norm = 1.0 / math.sqrt(seq * gw)
    for g in range(FNET_W // gw):
        a_h, a_l = _split_hi_lo(a_ref[:, g * gw:(g + 1) * gw])
        t = _dot3(a_h, a_l, ch_h, ch_l)
        tc_h, tc_l = _split_hi_lo(t[:, :gw])
        ts_h, ts_l = _split_hi_lo(t[:, gw:])
        y = _dot3(pc_h, pc_l, tc_h, tc_l) - _dot3(ps_h, ps_l, ts_h, ts_l)
        ya_ref[:, g * gw:(g + 1) * gw] = (y * norm).astype(ya_ref.dtype)

    hd = HEAD_DIM
    for hk in range(WIN_KV_HEADS):
        q = _stacked_q((bq0_ref, bq1_ref), hk)
        s = _dot_nt(q, bk_ref[:, hk * hd:(hk + 1) * hd])
        sink = _sink_column(sink_ref, hk, seq)
        m = _row_max([s], sink)
        e = jnp.exp(s - m)
        den = e.sum(axis=-1, keepdims=True) + jnp.exp(sink - m)
        o = jnp.dot(e.astype(BF16), bv_ref[:, hk * hd:(hk + 1) * hd], preferred_element_type=F32) / den
        for g in range(WIN_GROUP):
            h = hk * WIN_GROUP + g
            yb_ref[:, h * hd:(h + 1) * hd] = o[g * seq:(g + 1) * seq].astype(yb_ref.dtype)

    for h in range(NA_HEADS):
        hs = slice(h * hd, (h + 1) * hd)
        s = _dot_nt(cq_ref[:, hs], ck_ref[:, hs])
        m = s.max(axis=-1, keepdims=True)
        e = jnp.exp(s - m)
        den = e.sum(axis=-1, keepdims=True)
        o = jnp.dot(e.astype(BF16), cv_ref[:, hs], preferred_element_type=F32) / den
        yc_ref[:, hs] = o.astype(yc_ref.dtype)


def ctx_mixers(a32, p16, sink, tabs, *, batch, seq):
    def col(width, offset):
        return pl.BlockSpec((seq, width), lambda b: (b, offset // width))

    const = lambda shape: pl.BlockSpec(shape, lambda b: (0,) * len(shape))
    half_q = WIN_Q_W // 2
    in_specs = [
        col(FNET_W, 0), col(half_q, COL_BQ), col(half_q, COL_BQ + half_q),
        col(WIN_KV_W, COL_BK), col(WIN_KV_W, COL_BV),
        col(NA_W, COL_CQ), col(NA_W, COL_CK), col(NA_W, COL_CV),
        pl.BlockSpec(memory_space=pltpu.SMEM),
        const(tabs["ch_h"].shape), const(tabs["ch_l"].shape),
        const(tabs["pc_h"].shape), const(tabs["pc_l"].shape),
        const(tabs["ps_h"].shape), const(tabs["ps_l"].shape),
    ]
    out_specs = [pl.BlockSpec((seq, FNET_W), lambda b: (b, 0)),
                 pl.BlockSpec((seq, WIN_Q_W), lambda b: (b, 0)),
                 pl.BlockSpec((seq, NA_W), lambda b: (b, 0))]
    m = batch * seq
    out_shape = [jax.ShapeDtypeStruct((m, FNET_W), BF16), jax.ShapeDtypeStruct((m, WIN_Q_W), BF16),
                 jax.ShapeDtypeStruct((m, NA_W), BF16)]
    est = 2 * _nbytes((seq, IN_W), F32) + 2 * _nbytes((seq, D_MODEL), BF16) + (16 << 20)
    return pl.pallas_call(
        functools.partial(_ctx_mixer_kernel, seq=seq),
        grid=(batch,),
        in_specs=in_specs,
        out_specs=out_specs,
        out_shape=out_shape,
        compiler_params=_cparams(("arbitrary",), est),
        name="ctx_mixers",
    )(a32, p16, p16, p16, p16, p16, p16, p16, sink,
      tabs["ch_h"], tabs["ch_l"], tabs["pc_h"], tabs["pc_l"], tabs["ps_h"], tabs["ps_l"])


def _group_dot3(x, tab_h, tab_l):
    gw = FNET_GROUP_W
    outs = []
    for g in range(FNET_W // gw):
        x_h, x_l = _split_hi_lo(x[:, g * gw:(g + 1) * gw])
        outs.append(_dot3(x_h, x_l, tab_h, tab_l))
    return jnp.concatenate(outs, axis=1)


def _shifted_reverse(jrev_ref, x_hi, x_lo):
    out = jnp.dot(jrev_ref[...], x_hi, preferred_element_type=F32)
    if x_lo is not None:
        out += jnp.dot(jrev_ref[...], x_lo, preferred_element_type=F32)
    return out


def _dft_fold_kernel(a_ref, am_ref, nb_ref, mid_ref, jrev_ref, ch_h_ref, ch_l_ref, te_ref, to_ref, aux_ref):
    i = pl.program_id(1)
    gw = FNET_GROUP_W
    cc_h, cc_l = ch_h_ref[:, :gw], ch_l_ref[:, :gw]
    sc_h, sc_l = ch_h_ref[:, gw:], ch_l_ref[:, gw:]
    a = a_ref[...]
    tm = a.shape[0]
    row = lax.broadcasted_iota(jnp.int32, (tm, 1), 0)
    am_h, am_l = _split_hi_lo(am_ref[...])
    first = jnp.where(i > 0, nb_ref[0:1, :], 0.0)
    arev = jnp.where(row == 0, first, _shifted_reverse(jrev_ref, am_h, am_l))
    e = a + arev
    o = jnp.where((row == 0) & (i == 0), 0.0, a - arev)

    te_h, te_l = _split_hi_lo(_group_dot3(e, cc_h, cc_l))
    to_h, to_l = _split_hi_lo(_group_dot3(o, sc_h, sc_l))
    te_ref[0], te_ref[1], te_ref[2] = te_h, te_h, te_l
    to_ref[0], to_ref[1], to_ref[2] = to_h, to_h, to_l

    @pl.when(i == 0)
    def _():
        rows8 = lax.broadcasted_iota(jnp.int32, aux_ref.shape, 0)
        aux_ref[...] = _group_dot3(jnp.where(rows8 == 0, mid_ref[0:1, :], 0.0), cc_h, cc_l)


def dft_fold(a32, tabs, *, batch, seq, tm=DFT_FOLD_TM):
    half = seq // 2
    nt = half // tm
    tiles = seq // tm
    sub = 8
    w = FNET_W
    slab = pl.BlockSpec((3, tm, w), lambda b, i: (0, i, b))
    const = lambda arr: pl.BlockSpec(arr.shape, lambda b, i: (0, 0))
    jrev = tabs["jrev_fold"]
    est = 6 * _nbytes((tm, w), F32) + 4 * 3 * _nbytes((tm, w), BF16) + 2 * _nbytes(jrev.shape, BF16) + (12 << 20)
    return pl.pallas_call(
        _dft_fold_kernel,
        grid=(batch, nt),
        in_specs=[pl.BlockSpec((tm, w), lambda b, i: (b * tiles + i, 0)),
                  pl.BlockSpec((tm, w), lambda b, i: (b * tiles + tiles - 1 - i, 0)),
                  pl.BlockSpec((sub, w), lambda b, i: ((b * seq + (seq - i * tm) % seq) // sub, 0)),
                  pl.BlockSpec((sub, w), lambda b, i: ((b * seq + half) // sub, 0)),
                  const(jrev), const(tabs["ch_h"]), const(tabs["ch_l"])],
        out_specs=[slab, slab, pl.BlockSpec((None, sub, w), lambda b, i: (b, 0, 0))],
        out_shape=[jax.ShapeDtypeStruct((3, half, batch * w), BF16),
                   jax.ShapeDtypeStruct((3, half, batch * w), BF16),
                   jax.ShapeDtypeStruct((batch, sub, w), F32)],
        compiler_params=_cparams(("arbitrary", "arbitrary"), est),
        name="dft_fold",
    )(a32, a32, a32, a32, jrev, tabs["ch_h"], tabs["ch_l"])


def _dft_pos_kernel(c3_ref, s3_ref, c3x_ref, s3x_ref, te_ref, to_ref, aux_ref, jrev_ref, lo_ref, hi_ref,
                    accc_ref, accs_ref, accx_ref, *, nk, norm):
    k = pl.program_id(2)

    def dots():
        te, to = te_ref[...], to_ref[...]
        return (jnp.dot(c3_ref[...], te, preferred_element_type=F32),
                jnp.dot(s3_ref[...], to, preferred_element_type=F32),
                jnp.dot(c3x_ref[...], te, preferred_element_type=F32)
                + jnp.dot(s3x_ref[...], to, preferred_element_type=F32))

    @pl.when(k == 0)
    def _():
        accc_ref[...], accs_ref[...], accx_ref[...] = dots()

    @pl.when((k > 0) & (k < nk - 1))
    def _():
        dc, ds, dx = dots()
        accc_ref[...] += dc
        accs_ref[...] += ds
        accx_ref[...] += dx

    @pl.when(k == nk - 1)
    def _():
        dc, ds, dx = dots()
        tm = dc.shape[0]
        row = lax.broadcasted_iota(jnp.int32, (tm, 1), 0)
        mid_c = aux_ref[0:1, :]
        pc = accc_ref[...] + dc + jnp.where(row % 2 == 0, mid_c, -mid_c)
        ps = accs_ref[...] + ds
        lo_ref[...] = ((pc - ps) * norm).astype(lo_ref.dtype)
        mirrored = _shifted_reverse(jrev_ref, ((pc + ps) * norm).astype(BF16), None)
        edge = ((accx_ref[...] + dx)[0:1, :] + mid_c) * norm
        hi_ref[...] = jnp.where(row == 0, edge, mirrored).astype(hi_ref.dtype)


def dft_pos(te, to, aux, tabs, *, batch, seq, tm=DFT_POS_TM, tk=1024):
    half = seq // 2
    w = FNET_W
    kdim = 3 * half
    nk = kdim // tk
    ni = half // tm
    sub = 8
    norm = 1.0 / math.sqrt(seq * FNET_GROUP_W)
    jrev = tabs["jrev_pos"]
    tab_spec = pl.BlockSpec((tm, tk), lambda i, j, k: (i, k))
    edge_spec = pl.BlockSpec((None, sub, tk), lambda i, j, k: (i, 0, k))
    slab_spec = pl.BlockSpec((tk, w), lambda i, j, k: (k, j))
    est = 4 * _nbytes((tm, tk), BF16) + 4 * _nbytes((tk, w), BF16) + 5 * _nbytes((tm, w), F32)
    est += 4 * _nbytes((tm, w), BF16) + 2 * _nbytes(jrev.shape, BF16) + (2 << 20)
    return pl.pallas_call(
        functools.partial(_dft_pos_kernel, nk=nk, norm=norm),
        grid=(ni, batch, nk),
        in_specs=[tab_spec, tab_spec, edge_spec, edge_spec, slab_spec, slab_spec,
                  pl.BlockSpec((None,) + aux.shape[1:], lambda i, j, k: (j, 0, 0)),
                  pl.BlockSpec(jrev.shape, lambda i, j, k: (0, 0))],
        out_specs=[pl.BlockSpec((None, tm, w), lambda i, j, k: (j, i, 0)),
                   pl.BlockSpec((None, tm, w), lambda i, j, k: (j, ni - 1 - i, 0))],
        out_shape=[jax.ShapeDtypeStruct((batch, half, w), BF16)] * 2,
        scratch_shapes=[pltpu.VMEM((tm, w), F32), pltpu.VMEM((tm, w), F32), pltpu.VMEM((sub, w), F32)],
        compiler_params=_cparams(("arbitrary", "arbitrary", "arbitrary"), est),
        name="dft_pos",
    )(tabs["c3"], tabs["s3"], tabs["c3x"], tabs["s3x"],
      te.reshape(kdim, batch * w), to.reshape(kdim, batch * w), aux, jrev)


def latent_fourier(a32, tabs, *, batch, seq):
    te, to, aux = dft_fold(a32, tabs, batch=batch, seq=seq)
    lo, hi = dft_pos(te, to, aux, tabs, batch=batch, seq=seq)
    return jnp.concatenate([lo, hi], axis=1).reshape(batch * seq, FNET_W)


def _win_attn_kernel(q0_ref, q1_ref, k_ref, v_ref, ck_ref, cv_ref, sink_ref, o_ref, *, seq):
    n = pl.program_id(1)
    blk = WIN
    span = 3 * blk
    hd = HEAD_DIM
    start = pl.multiple_of(jnp.clip((n - 1) * blk, 0, seq - span), blk)
    rows = WIN_GROUP * blk
    ipos = n * blk + lax.broadcasted_iota(jnp.int32, (rows, span), 0) % blk
    jpos = start + lax.broadcasted_iota(jnp.int32, (rows, span), 1)
    valid = jnp.abs(ipos - jpos) <= WIN

    for hk in range(WIN_KV_HEADS):
        hs = slice(hk * hd, (hk + 1) * hd)
        q = _stacked_q((q0_ref, q1_ref), hk)
        s = jnp.where(valid, _dot_nt(q, k_ref[pl.ds(start, span), hs]), NEG_INF)
        sc = _dot_nt(q, ck_ref[:, hs].astype(BF16))
        sink = _sink_column(sink_ref, hk, blk)
        m = _row_max([s, sc], sink)
        e = jnp.exp(s - m)
        ec = jnp.exp(sc - m)
        den = e.sum(axis=-1, keepdims=True) + ec.sum(axis=-1, keepdims=True) + jnp.exp(sink - m)
        o = jnp.dot(e.astype(BF16), v_ref[pl.ds(start, span), hs], preferred_element_type=F32)
        o += jnp.dot(ec.astype(BF16), cv_ref[:, hs].astype(BF16), preferred_element_type=F32)
        o = o / den
        for g in range(WIN_GROUP):
            h = hk * WIN_GROUP + g
            o_ref[:, h * hd:(h + 1) * hd] = o[g * blk:(g + 1) * blk].astype(o_ref.dtype)


def win_attention(p16, cache_k, cache_v, sink, *, batch, seq, layer):
    blk = WIN
    nblk = seq // blk
    past = cache_k.shape[2]
    half_q = WIN_Q_W // 2

    def q_spec(offset):
        return pl.BlockSpec((blk, half_q), lambda b, n: (b * nblk + n, offset // half_q))

    def seq_spec(offset):
        return pl.BlockSpec((seq, WIN_KV_W), lambda b, n: (b, offset // WIN_KV_W))

    cache_spec = pl.BlockSpec((None, None, past, WIN_KV_W), lambda b, n: (b, layer, 0, 0))
    in_specs = [q_spec(COL_BQ), q_spec(COL_BQ + half_q), seq_spec(COL_BK), seq_spec(COL_BV),
                cache_spec, cache_spec, pl.BlockSpec(memory_space=pltpu.SMEM)]
    est = 4 * _nbytes((seq, WIN_KV_W), BF16) + 4 * _nbytes((past, WIN_KV_W), F32) + (16 << 20)
    return pl.pallas_call(
        functools.partial(_win_attn_kernel, seq=seq),
        grid=(batch, nblk),
        in_specs=in_specs,
        out_specs=pl.BlockSpec((blk, WIN_Q_W), lambda b, n: (b * nblk + n, 0)),
        out_shape=jax.ShapeDtypeStruct((batch * seq, WIN_Q_W), BF16),
        compiler_params=_cparams(("arbitrary", "arbitrary"), est),
        name="win_attn",
    )(p16, p16, p16, p16, cache_k, cache_v, sink)


def _na_window_start(rg, rows_n):
    return jnp.clip(rg * NA_QROWS - NA_KH // 2, 0, rows_n - NA_KROWS)


def _na_attn_kernel(q_ref, k_ref, v_ref, ck_ref, cv_ref, bias_ref, o_ref, *, rows_n):
    rg = pl.program_id(1)
    hd = HEAD_DIM
    nkeys = NA_KROWS * GRID_W
    start = pl.multiple_of(_na_window_start(rg, rows_n) * GRID_W, GRID_W)
    for h in range(NA_HEADS):
        hs = slice(h * hd, (h + 1) * hd)
        q = q_ref[:, hs]
        s = _dot_nt(q, k_ref[pl.ds(start, nkeys), hs]) + bias_ref[h]
        sc = _dot_nt(q, ck_ref[:, hs].astype(BF16))
        m = _row_max([s, sc])
        e = jnp.exp(s - m)
        ec = jnp.exp(sc - m)
        den = e.sum(axis=-1, keepdims=True) + ec.sum(axis=-1, keepdims=True)
        o = jnp.dot(e.astype(BF16), v_ref[pl.ds(start, nkeys), hs], preferred_element_type=F32)
        o += jnp.dot(ec.astype(BF16), cv_ref[:, hs].astype(BF16), preferred_element_type=F32)
        o_ref[:, hs] = (o / den).astype(o_ref.dtype)


def na_attention(p16, cache_k, cache_v, bias_tab, *, batch, seq, layer):
    rows_n = seq // GRID_W
    ngroups = rows_n // NA_QROWS
    nq = NA_QROWS * GRID_W
    nkeys = NA_KROWS * GRID_W
    past = cache_k.shape[2]

    def seq_spec(offset):
        return pl.BlockSpec((seq, NA_W), lambda b, rg: (b, offset // NA_W), pipeline_mode=pl.Buffered(1))

    def pattern(b, rg):
        return (layer, (rg > 0).astype(jnp.int32) + (rg == ngroups - 1).astype(jnp.int32), 0, 0, 0)

    cache_spec = pl.BlockSpec((None, None, past, NA_W), lambda b, rg: (b, layer, 0, 0))
    in_specs = [pl.BlockSpec((nq, NA_W), lambda b, rg: (b * ngroups + rg, COL_CQ // NA_W)),
                seq_spec(COL_CK), seq_spec(COL_CV), cache_spec, cache_spec,
                pl.BlockSpec((None, None, NA_HEADS, nq, nkeys), pattern, pipeline_mode=pl.Buffered(1))]
    est = 2 * _nbytes((seq, NA_W), BF16) + _nbytes((NA_HEADS, nq, nkeys), F32)
    est += 4 * _nbytes((past, NA_W), F32) + (12 << 20)
    return pl.pallas_call(
        functools.partial(_na_attn_kernel, rows_n=rows_n),
        grid=(batch, ngroups),
        in_specs=in_specs,
        out_specs=pl.BlockSpec((nq, NA_W), lambda b, rg: (b * ngroups + rg, 0)),
        out_shape=jax.ShapeDtypeStruct((batch * seq, NA_W), BF16),
        compiler_params=_cparams(("arbitrary", "arbitrary"), est),
        name="na_attn",
    )(p16, p16, p16, cache_k, cache_v, bias_tab)


def _dft_cos_sin(n):
    idx = jnp.arange(n, dtype=jnp.int32)
    phase = (idx[:, None] * idx[None, :]) % n
    ang = phase.astype(F32) * (2.0 * math.pi / n)
    return jnp.cos(ang), jnp.sin(ang)


def _hi_lo(x):
    hi = x.astype(BF16)
    return hi, (x - hi.astype(F32)).astype(BF16)


def _fourier_tables(ctx_seq, lat_seq):
    cc, sc = _dft_cos_sin(FNET_GROUP_W)
    ch_h, ch_l = _hi_lo(jnp.concatenate([cc, sc], axis=1))
    cp, sp = _dft_cos_sin(ctx_seq)
    pc_h, pc_l = _hi_lo(cp)
    ps_h, ps_l = _hi_lo(sp)
    half = lat_seq // 2
    idx = jnp.arange(half, dtype=jnp.int32)

    def tables(rows):
        ang = ((rows[:, None] * idx[None, :]) % lat_seq).astype(F32) * (2.0 * math.pi / lat_seq)
        c_h, c_l = _hi_lo(jnp.cos(ang))
        s_h, s_l = _hi_lo(jnp.sin(ang))
        return jnp.concatenate([c_h, c_l, c_h], axis=1), jnp.concatenate([s_h, s_l, s_h], axis=1)

    c3, s3 = tables(idx)
    ni = half // DFT_POS_TM
    c3x, s3x = tables((jnp.arange(ni, dtype=jnp.int32) + 1) * DFT_POS_TM)
    pad = lambda t: jnp.pad(t[:, None, :], ((0, 0), (0, 7), (0, 0)))
    return dict(ch_h=ch_h, ch_l=ch_l, pc_h=pc_h, pc_l=pc_l, ps_h=ps_h, ps_l=ps_l, c3=c3, s3=s3,
                c3x=pad(c3x), s3x=pad(s3x),
                jrev_fold=_shifted_antidiagonal(DFT_FOLD_TM), jrev_pos=_shifted_antidiagonal(DFT_POS_TM))


def _shifted_antidiagonal(n):
    r = jnp.arange(n)
    return ((r[:, None] + r[None, :]) == n).astype(BF16)


def _rope_tables(seq):
    t = jnp.arange(seq)
    n_freq = HEAD_DIM // 4
    inv = ROPE_BASE ** (-jnp.arange(n_freq, dtype=F32) / n_freq)
    ang_r = (t // GRID_W).astype(F32)[:, None] * inv
    ang_c = (t % GRID_W).astype(F32)[:, None] * inv
    zeros = jnp.zeros_like(ang_r)
    cos = jnp.concatenate([jnp.cos(ang_r)] * 2 + [jnp.cos(ang_c)] * 2, axis=1)
    sin_lo = jnp.concatenate([-jnp.sin(ang_r), zeros, -jnp.sin(ang_c), zeros], axis=1)
    sin_hi = jnp.concatenate([zeros, jnp.sin(ang_r), zeros, jnp.sin(ang_c)], axis=1)
    return cos, sin_lo, sin_hi


def _na_bias_tables(rpb, rows_n):
    depth = rpb.shape[0]
    col = jnp.arange(GRID_W)
    cs = jnp.clip(col - NA_KW // 2, 0, GRID_W - NA_KW)
    col_ok = (col[None, :] >= cs[:, None]) & (col[None, :] < cs[:, None] + NA_KW)
    idx_c = jnp.clip(col[None, :] - col[:, None], -(NA_KW - 1), NA_KW - 1) + NA_KW - 1
    bt = jnp.where(col_ok[None, None, None], rpb.astype(F32)[:, :, :, idx_c], NEG_INF)

    ngroups = rows_n // NA_QROWS
    groups = jnp.array([0, 1, ngroups - 1])
    r = groups[:, None] * NA_QROWS + jnp.arange(NA_QROWS)[None, :]
    ws = jnp.clip(groups * NA_QROWS - NA_KH // 2, 0, rows_n - NA_KROWS)
    krow = ws[:, None] + jnp.arange(NA_KROWS)[None, :]
    rs = jnp.clip(r - NA_KH // 2, 0, rows_n - NA_KH)
    row_ok = (krow[:, None, :] >= rs[:, :, None]) & (krow[:, None, :] < rs[:, :, None] + NA_KH)
    rho = jnp.clip(krow[:, None, :] - r[:, :, None] + NA_KH - 1, 0, 2 * NA_KH - 2)
    tab = jnp.where(row_ok[None, None, :, :, :, None, None], bt[:, :, rho], NEG_INF)
    tab = jnp.transpose(tab, (0, 2, 1, 3, 5, 4, 6))
    return tab.reshape(depth, 3, NA_HEADS, NA_QROWS * GRID_W, NA_KROWS * GRID_W)


def kernel(x_prompt, x_sample, cache_win_k, cache_win_v, cache_na_k, cache_na_v, c, c_ctx, w_mod, b_mod,
           norm1_g, w_in, win_sink, na_rpb, w_a, w_b, w_c, w_gate, b_gate, w_out, norm2_g, w1, w2, final_g):
    batch, seq, d = x_prompt.shape
    dec_batch, dec_seq, _ = x_sample.shape
    past = cache_win_k.shape[2]
    m_ctx, m_lat = batch * seq, dec_batch * dec_seq

    cvec = jnp.concatenate([c_ctx[None], c, jnp.zeros((N_MOD - 1 - dec_batch, d), F32)], axis=0)
    mods = adaln_all(cvec, w_mod, b_mod).reshape(DEPTH, N_MOD, 6, 1, d)
    shifts = jnp.stack([mods[:, :, 0, 0], mods[:, :, 3, 0]], axis=0)
    shifts = jnp.pad(shifts, ((0, 0), (0, 0), (0, MOD_PAD_ROWS - N_MOD), (0, 0))).astype(BF16)

    w_in_b, sh1_w_in_all = cast_bias(w_in, shifts[0])
    w_gate_b, sh1_w_gate_all = cast_bias(w_gate, shifts[0])
    w_a_b, w_b_b, w_c_b, w_out_b = w_a.astype(BF16), w_b.astype(BF16), w_c.astype(BF16), w_out.astype(BF16)
    w1_cur, w2_cur = w1[0].astype(BF16), w2[0].astype(BF16)

    ftabs = _fourier_tables(seq, dec_seq)
    rope_tabs = _rope_tables(dec_seq)
    na_bias = _na_bias_tables(na_rpb, dec_seq // GRID_W)
    cwk, cwv = (t.reshape(dec_batch, DEPTH, past, WIN_KV_W) for t in (cache_win_k, cache_win_v))
    cnk, cnv = (t.reshape(dec_batch, DEPTH, past, NA_W) for t in (cache_na_k, cache_na_v))
    cache_shapes = [(batch, DEPTH, seq, WIN_KV_W)] * 2 + [(batch, DEPTH, seq, NA_W)] * 2

    group_mods = (ModRows(0, m_ctx), ModRows(1, dec_seq))
    xs_by_group = [x_prompt.reshape(m_ctx, d), x_sample.reshape(m_lat, d)]
    pending = [prep_norm(x, norm1_g[0], mods[0, :, 1], mod) for x, mod in zip(xs_by_group, group_mods)]
    new_caches = None
    for l in range(DEPTH):
        sh1_w_in, sh1_w_gate = sh1_w_in_all[l], sh1_w_gate_all[l]
        sh2_w1 = mod_bias(shifts[1, l], w1_cur, None)
        gate1, scale2, gate2 = mods[l, :, 2], mods[l, :, 4], mods[l, :, 5]
        last = l + 1 == DEPTH
        next_mlp = [None, None]
        for gi, mod in enumerate(group_mods):
            x = xs_by_group[gi]
            xg, ssq = pending[gi]
            if gi == 0:
                a32, p16, *new_caches = in_projection(xg, ssq, w_in_b, sh1_w_in, l, mod, seq=seq,
                                                      cache_shapes=cache_shapes, prev_caches=new_caches, tm=512)
                ya, yb, yc = ctx_mixers(a32, p16, win_sink[l], ftabs, batch=batch, seq=seq)
            else:
                a32, p16 = in_projection(xg, ssq, w_in_b, sh1_w_in, l, mod, rope_tabs, seq=dec_seq)
                ya = latent_fourier(a32, ftabs, batch=dec_batch, seq=dec_seq)
                yb = win_attention(p16, cwk, cwv, win_sink[l], batch=dec_batch, seq=dec_seq, layer=l)
                yc = na_attention(p16, cnk, cnv, na_bias, batch=dec_batch, seq=dec_seq, layer=l)
            mrg = merge(xg, ssq, ya, yb, yc, w_gate_b, b_gate, sh1_w_gate, w_a_b, w_b_b, w_c_b, l, mod)
            x, xg2, ssq2 = matmul_resid(mrg, w_out_b, l, x, gate1, mod, next_norm=(norm2_g[l], scale2), tn=512)
            if last:
                u = mlp_up(xg2, ssq2, w1_cur, sh2_w1, None, mod)
                x = matmul_resid(u, w2_cur, None, x, gate2, mod, tk=2048)
            else:
                u, next_mlp[gi] = mlp_up(xg2, ssq2, w1_cur, sh2_w1, None, mod, side_cast=((w1, w2)[gi], l + 1))
                x, xg, ssq = matmul_resid(u, w2_cur, None, x, gate2, mod, tk=2048,
                                          next_norm=(norm1_g[l + 1], mods[l + 1, :, 1]))
                pending[gi] = (xg, ssq)
            xs_by_group[gi] = x
        w1_cur, w2_cur = next_mlp

    y_prompt = final_norm(xs_by_group[0], final_g).reshape(batch, seq, d)
    y_sample = final_norm(xs_by_group[1], final_g).reshape(dec_batch, dec_seq, d)
    wk, wv, nk, nv = new_caches
    kv_shape = lambda heads: (batch, DEPTH, seq, heads, HEAD_DIM)
    return (y_prompt, y_sample, wk.reshape(kv_shape(WIN_KV_HEADS)), wv.reshape(kv_shape(WIN_KV_HEADS)),
            nk.reshape(kv_shape(NA_HEADS)), nv.reshape(kv_shape(NA_HEADS)))
```

```python
import functools
import math

import jax
import jax.numpy as jnp
from jax import lax
from jax.experimental import pallas as pl
from jax.experimental.pallas import tpu as pltpu

F32 = jnp.float32
BF16 = jnp.bfloat16

D_MODEL = 4096
DEPTH = 4
GRID_W = 64
HEAD_DIM = 128
FNET_GROUP_W = 256
FNET_W = 1024
WIN_Q_HEADS = 16
WIN_KV_HEADS = 4
WIN_GROUP = WIN_Q_HEADS // WIN_KV_HEADS
WIN = 128
WIN_Q_W = WIN_Q_HEADS * HEAD_DIM
WIN_KV_W = WIN_KV_HEADS * HEAD_DIM
NA_HEADS = 8
NA_KH = 8
NA_KW = 16
NA_W = NA_HEADS * HEAD_DIM
MLP_HIDDEN = 4 * D_MODEL
IN_W = 7168
ROPE_BASE = 10000.0
EPS = 1e-6
NEG_INF = -1e30
ATTN_SCALE = HEAD_DIM ** -0.5
N_MOD = 8
MOD_PAD_ROWS = 16

COL_A, COL_BQ, COL_BK, COL_BV, COL_CQ, COL_CK, COL_CV = 0, 1024, 3072, 3584, 4096, 5120, 6144
PROJ_TN = 1024

NA_QROWS = 4
NA_KROWS = NA_QROWS + NA_KH

DFT_FOLD_TM = 512
DFT_POS_TM = 1024

V7X_SCOPED_VMEM_BYTES = 60000 * 1024
VMEM_MARGIN_BYTES = 6 << 20


def _cparams(semantics, est_bytes):
    limit = min(V7X_SCOPED_VMEM_BYTES, int(est_bytes) + VMEM_MARGIN_BYTES)
    return pltpu.CompilerParams(dimension_semantics=semantics, vmem_limit_bytes=limit)


def _nbytes(shape, dtype):
    return math.prod(shape) * jnp.dtype(dtype).itemsize


def _adaln_kernel(c_ref, w_ref, b_ref, o_ref):
    c = c_ref[...]
    s = (c * (1.0 / (1.0 + jnp.exp(-c)))).astype(BF16)
    o_ref[...] = jnp.dot(s, w_ref[...].astype(BF16), preferred_element_type=F32) + b_ref[...]


def adaln_all(cvec, w_mod, b_mod, tn=512):
    depth, d, n = w_mod.shape
    est = 2 * _nbytes((d, tn), F32) + _nbytes((d, tn), BF16) + 4 * _nbytes((N_MOD, d), F32)
    return pl.pallas_call(
        _adaln_kernel,
        grid=(depth, n // tn),
        in_specs=[
            pl.BlockSpec((N_MOD, d), lambda l, j: (0, 0)),
            pl.BlockSpec((None, d, tn), lambda l, j: (l, 0, j)),
            pl.BlockSpec((None, 1, tn), lambda l, j: (l, 0, j)),
        ],
        out_specs=pl.BlockSpec((None, N_MOD, tn), lambda l, j: (l, 0, j)),
        out_shape=jax.ShapeDtypeStruct((depth, N_MOD, n), F32),
        compiler_params=_cparams(("arbitrary", "arbitrary"), est),
        name="adaln",
    )(cvec, w_mod, b_mod.reshape(depth, 1, n))


SSQ_LANES = 128


class ModRows:
    def __init__(self, offset, rows_per_mod):
        self.offset = offset
        self.rows_per_mod = rows_per_mod

    def index(self, i, tm):
        return self.offset + (i * tm) // self.rows_per_mod


def _row_scale(ssq_ref, d):
    return lax.rsqrt(ssq_ref[:, 0:1] * (1.0 / d) + EPS)


def _row_ssq(x, shape):
    return jnp.broadcast_to(jnp.sum(x * x, axis=-1, keepdims=True), shape)


def _prep_kernel(x_ref, g_ref, sc_ref, xg_ref, ssq_ref):
    x = x_ref[...]
    xg_ref[...] = (x * (g_ref[...] * (1.0 + sc_ref[...]))).astype(xg_ref.dtype)
    ssq_ref[...] = _row_ssq(x, ssq_ref.shape)


def prep_norm(x, g, sc, mod, *, tr=256):
    m, d = x.shape
    est = 2 * _nbytes((tr, d), F32) + 2 * _nbytes((tr, d), BF16) + 3 * _nbytes((tr, d), F32)
    return pl.pallas_call(
        _prep_kernel,
        grid=(m // tr,),
        in_specs=[pl.BlockSpec((tr, d), lambda i: (i, 0)),
                  pl.BlockSpec((1, d), lambda i: (0, 0)),
                  pl.BlockSpec((None, 1, d), lambda i: (mod.index(i, tr), 0, 0))],
        out_specs=[pl.BlockSpec((tr, d), lambda i: (i, 0)), pl.BlockSpec((tr, SSQ_LANES), lambda i: (i, 0))],
        out_shape=[jax.ShapeDtypeStruct((m, d), BF16), jax.ShapeDtypeStruct((m, SSQ_LANES), F32)],
        compiler_params=_cparams(("arbitrary",), est),
        name="prep_norm",
    )(x, g.reshape(1, d), sc)


def _cast_bias_kernel(w_ref, sh_ref, wb_ref, o_ref):
    wb = w_ref[...].astype(BF16)
    wb_ref[...] = wb
    o_ref[...] = jnp.dot(sh_ref[...], wb, preferred_element_type=F32)


def cast_bias(w, sh, *, layers, tn=512):
    _, d, n = w.shape
    depth = layers
    rows = sh.shape[1]
    est = 2 * _nbytes((d, tn), F32) + 3 * _nbytes((d, tn), BF16) + 4 * _nbytes((rows, d), BF16)
    wb, out = pl.pallas_call(
        _cast_bias_kernel,
        grid=(depth, n // tn),
        in_specs=[pl.BlockSpec((None, d, tn), lambda l, j: (l, 0, j)),
                  pl.BlockSpec((None, rows, d), lambda l, j: (l, 0, 0))],
        out_specs=[pl.BlockSpec((None, d, tn), lambda l, j: (l, 0, j)),
                   pl.BlockSpec((None, rows, tn), lambda l, j: (l, 0, j))],
        out_shape=[jax.ShapeDtypeStruct((depth, d, n), BF16), jax.ShapeDtypeStruct((depth, rows, n), F32)],
        compiler_params=_cparams(("arbitrary", "arbitrary"), est),
        name="cast_bias",
    )(w, sh)
    return wb, out.reshape(depth, rows, 1, n)


def _final_norm_kernel(x_ref, g_ref, o_ref):
    x = x_ref[...]
    y = x * lax.rsqrt(jnp.mean(x * x, axis=-1, keepdims=True) + EPS)
    o_ref[...] = (y * g_ref[...]).astype(o_ref.dtype)


def final_norm(x, g, *, tr=256):
    m, d = x.shape
    est = 4 * _nbytes((tr, d), F32) + 3 * _nbytes((tr, d), F32)
    return pl.pallas_call(
        _final_norm_kernel,
        grid=(m // tr,),
        in_specs=[pl.BlockSpec((tr, d), lambda i: (i, 0)), pl.BlockSpec((1, d), lambda i: (0, 0))],
        out_specs=pl.BlockSpec((tr, d), lambda i: (i, 0)),
        out_shape=jax.ShapeDtypeStruct((m, d), F32),
        compiler_params=_cparams(("arbitrary",), est),
        name="final_norm",
    )(x, g.reshape(1, d))


def _mod_bias_kernel(sh_ref, w_ref, o_ref):
    o_ref[...] = jnp.dot(sh_ref[...], w_ref[...], preferred_element_type=F32)


def mod_bias(sh, w, layer, *, tn=1024):
    rows, d = sh.shape
    n = w.shape[-1]
    est = 2 * _nbytes((d, tn), BF16) + 4 * _nbytes((rows, d), BF16) + 4 * _nbytes((rows, tn), F32)
    out = pl.pallas_call(
        _mod_bias_kernel,
        grid=(n // tn,),
        in_specs=[pl.BlockSpec((rows, d), lambda j: (0, 0)),
                  _weight_spec(w, layer, d, tn, lambda j: (0, j))],
        out_specs=pl.BlockSpec((rows, tn), lambda j: (0, j)),
        out_shape=jax.ShapeDtypeStruct((rows, n), F32),
        compiler_params=_cparams(("arbitrary",), est),
        name="mod_bias",
    )(sh, w)
    return out.reshape(rows, 1, n)


def _weight_spec(w, layer, rows, tn, index):
    if layer is None:
        return pl.BlockSpec((rows, tn), index)
    return pl.BlockSpec((None, rows, tn), lambda *ids: (layer,) + tuple(index(*ids)))


def _mlp_up_kernel(*refs, n_side):
    xg_ref, w_ref, ssq_ref, shw_ref = refs[:4]
    src_refs = refs[4:4 + n_side]
    o_ref = refs[4 + n_side]
    dst_refs = refs[5 + n_side:]
    acc = jnp.dot(xg_ref[...], w_ref[...], preferred_element_type=F32)
    a = jnp.maximum(_row_scale(ssq_ref, xg_ref.shape[1]) * acc + shw_ref[...], 0.0)
    o_ref[...] = (a * a).astype(o_ref.dtype)
    for src_ref, dst_ref in zip(src_refs, dst_refs):
        dst_ref[...] = src_ref[...].astype(dst_ref.dtype)


def mlp_up(xg, ssq, w, shw, layer, mod, *, side_casts=(), tm=1024, tn=1024):
    m, d = xg.shape
    n = w.shape[-1]
    nj = n // tn
    steps = (m // tm) * nj
    in_specs = [pl.BlockSpec((tm, d), lambda i, j: (i, 0)),
                _weight_spec(w, layer, d, tn, lambda i, j: (0, j)),
                pl.BlockSpec((tm, SSQ_LANES), lambda i, j: (i, 0)),
                pl.BlockSpec((None, 1, tn), lambda i, j: (mod.index(i, tm), 0, j))]
    args = [xg, w, ssq, shw]
    out_specs = [pl.BlockSpec((tm, tn), lambda i, j: (i, j))]
    out_shape = [jax.ShapeDtypeStruct((m, n), BF16)]
    est = 2 * _nbytes((tm, d), BF16) + 2 * _nbytes((d, tn), BF16) + 2 * _nbytes((tm, tn), BF16)
    est += 2 * _nbytes((tm, tn), F32) + 2 * _nbytes((tm, SSQ_LANES), F32)
    final_shapes = []
    for side_cast in side_casts:
        in_spec, arg, out_spec, shape, vmem, final_shape = _side_cast_operands(
            side_cast, steps, lambda i, j: i * nj + j)
        in_specs.append(in_spec)
        args.append(arg)
        out_specs.append(out_spec)
        out_shape.append(shape)
        final_shapes.append(final_shape)
        est += vmem
    outs = pl.pallas_call(
        functools.partial(_mlp_up_kernel, n_side=len(side_casts)),
        grid=(m // tm, nj),
        in_specs=in_specs,
        out_specs=out_specs,
        out_shape=out_shape,
        compiler_params=_cparams(("arbitrary", "arbitrary"), est),
        name="mlp_up",
    )(*args)
    if not side_casts:
        return outs[0]
    return outs[0], [o.reshape(shape) for o, shape in zip(outs[1:], final_shapes)]


def _mm_resid_kernel(*refs, nk, emit):
    x_ref, w_ref, r_ref, gate_ref = refs[:4]
    pos = 4
    if emit:
        gn_ref, scn_ref = refs[pos:pos + 2]
        pos += 2
    o_ref = refs[pos]
    pos += 1
    if emit:
        xg_ref, ssq_ref = refs[pos:pos + 2]
        pos += 2
    j = pl.program_id(1)

    def part():
        return jnp.dot(x_ref[...], w_ref[...], preferred_element_type=F32)

    def finish(acc):
        xn = r_ref[...] + gate_ref[...] * acc
        o_ref[...] = xn
        if emit:
            xg_ref[...] = (xn * (gn_ref[...] * (1.0 + scn_ref[...]))).astype(xg_ref.dtype)
            ssq = _row_ssq(xn, ssq_ref.shape)

            @pl.when(j == 0)
            def _():
                ssq_ref[...] = ssq

            @pl.when(j > 0)
            def _():
                ssq_ref[...] += ssq

    if nk == 1:
        finish(part())
        return
    acc_ref = refs[pos]
    k = pl.program_id(2)

    @pl.when(k == 0)
    def _():
        acc_ref[...] = part()

    @pl.when((k > 0) & (k < nk - 1))
    def _():
        acc_ref[...] += part()

    @pl.when(k == nk - 1)
    def _():
        finish(acc_ref[...] + part())


def matmul_resid(x, w, layer, resid, gate, mod, *, next_norm=None, tm=1024, tn=1024, tk=None):
    m, kdim = x.shape
    n = w.shape[-1]
    tk = kdim if tk is None else tk
    nk = kdim // tk
    emit = next_norm is not None
    mod_spec = pl.BlockSpec((None, 1, tn), lambda i, j, k: (mod.index(i, tm), 0, j))
    tile_spec = pl.BlockSpec((tm, tn), lambda i, j, k: (i, j))
    in_specs = [pl.BlockSpec((tm, tk), lambda i, j, k: (i, k)),
                _weight_spec(w, layer, tk, tn, lambda i, j, k: (k, j)),
                tile_spec, mod_spec]
    args = [x, w, resid, gate]
    out_specs = [tile_spec]
    out_shape = [jax.ShapeDtypeStruct((m, n), F32)]
    est = 2 * _nbytes((tm, tk), BF16) + 2 * _nbytes((tk, tn), BF16) + 6 * _nbytes((tm, tn), F32)
    if emit:
        g, sc = next_norm
        in_specs += [pl.BlockSpec((1, tn), lambda i, j, k: (0, j)), mod_spec]
        args += [g.reshape(1, n), sc]
        out_specs += [tile_spec, pl.BlockSpec((tm, SSQ_LANES), lambda i, j, k: (i, 0))]
        out_shape += [jax.ShapeDtypeStruct((m, n), BF16), jax.ShapeDtypeStruct((m, SSQ_LANES), F32)]
        est += 2 * _nbytes((tm, tn), BF16) + 2 * _nbytes((tm, SSQ_LANES), F32)
    if nk > 1:
        est += _nbytes((tm, tn), F32)
    outs = pl.pallas_call(
        functools.partial(_mm_resid_kernel, nk=nk, emit=emit),
        grid=(m // tm, n // tn, nk),
        in_specs=in_specs,
        out_specs=out_specs,
        out_shape=out_shape,
        scratch_shapes=[pltpu.VMEM((tm, tn), F32)] if nk > 1 else [],
        compiler_params=_cparams(("arbitrary", "arbitrary", "arbitrary"), est),
        name="mm_resid",
    )(*args)
    return outs if emit else outs[0]


def _rope(x, cos, sin_lo, sin_hi):
    quarter = HEAD_DIM // 4
    return (x * cos + pltpu.roll(x, HEAD_DIM - quarter, 1) * sin_lo + pltpu.roll(x, quarter, 1) * sin_hi)


def _proj_tile_plan(rope, caches):
    heads_per_tile = PROJ_TN // HEAD_DIM
    plan = {}
    for j in range(IN_W // PROJ_TN):
        ops = []
        for hh in range(heads_per_tile):
            col = j * PROJ_TN + hh * HEAD_DIM
            is_bq = COL_BQ <= col < COL_BK
            is_bk = COL_BK <= col < COL_BV
            is_cq = COL_CQ <= col < COL_CK
            ops.append((rope and (is_bq or is_bk), is_bq or is_cq))
        col0 = j * PROJ_TN
        f32_use = None
        if col0 == COL_A:
            f32_use = "a"
        elif caches and col0 == COL_BK:
            f32_use = "win_kv"
        elif caches and col0 == COL_CK:
            f32_use = "na_k"
        elif caches and col0 == COL_CV:
            f32_use = "na_v"
        plan.setdefault((tuple(ops), f32_use), []).append(j)
    return plan


def _proj_kernel(*refs, rope, caches, n_alias):
    refs = list(refs)
    xg_ref, w_ref, ssq_ref, shw_ref = refs[:4]
    pos = 4
    if rope:
        cos_ref, slo_ref, shi_ref = refs[pos:pos + 3]
        pos += 3
    pos += n_alias
    a32_ref, o16_ref = refs[pos:pos + 2]
    if caches:
        wk_ref, wv_ref, nk_ref, nv_ref = refs[pos + 2:pos + 6]
    j = pl.program_id(1)

    def run(ops, f32_use):
        acc = jnp.dot(xg_ref[...], w_ref[...], preferred_element_type=F32)
        acc = _row_scale(ssq_ref, xg_ref.shape[1]) * acc + shw_ref[...]
        if f32_use == "a":
            a32_ref[...] = acc
        elif f32_use == "win_kv":
            wk_ref[...] = acc[:, :WIN_KV_W].reshape(wk_ref.shape)
            wv_ref[...] = acc[:, WIN_KV_W:].reshape(wv_ref.shape)
        elif f32_use == "na_k":
            nk_ref[...] = acc.reshape(nk_ref.shape)
        elif f32_use == "na_v":
            nv_ref[...] = acc.reshape(nv_ref.shape)
        for hh, (do_rope, do_scale) in enumerate(ops):
            cols = slice(hh * HEAD_DIM, (hh + 1) * HEAD_DIM)
            x = acc[:, cols]
            if do_rope:
                x = _rope(x, cos_ref[...], slo_ref[...], shi_ref[...])
            if do_scale:
                x = x * ATTN_SCALE
            o16_ref[:, cols] = x.astype(BF16)

    for (ops, f32_use), tiles in _proj_tile_plan(rope, caches).items():
        cond = j == tiles[0]
        for t in tiles[1:]:
            cond = cond | (j == t)
        pl.when(cond)(functools.partial(run, ops, f32_use))


def in_projection(xg, ssq, w, shw, layer, mod, rope_tabs=None, *, seq, cache_shapes=None, prev_caches=None,
                  tm=1024):
    m, d = xg.shape
    n = w.shape[-1]
    tn = PROJ_TN
    rope = rope_tabs is not None
    caches = cache_shapes is not None
    in_specs = [pl.BlockSpec((tm, d), lambda i, j: (i, 0)),
                _weight_spec(w, layer if w.ndim == 3 else None, d, tn, lambda i, j: (0, j)),
                pl.BlockSpec((tm, SSQ_LANES), lambda i, j: (i, 0)),
                pl.BlockSpec((None, 1, tn), lambda i, j: (mod.index(i, tm), 0, j))]
    args = [xg, w, ssq, shw]
    if rope:
        tiles_per_seq = seq // tm
        tab_spec = pl.BlockSpec((tm, HEAD_DIM), lambda i, j: (i % tiles_per_seq, 0))
        in_specs += [tab_spec] * 3
        args += list(rope_tabs)
    n_alias = 0
    aliases = {}
    if prev_caches is not None:
        n_alias = len(prev_caches)
        for t, arr in enumerate(prev_caches):
            aliases[len(args)] = 2 + t
            in_specs.append(pl.BlockSpec(memory_space=pl.ANY))
            args.append(arr)
    out_specs = [pl.BlockSpec((tm, FNET_W), lambda i, j: (i, 0)), pl.BlockSpec((tm, tn), lambda i, j: (i, j))]
    out_shape = [jax.ShapeDtypeStruct((m, FNET_W), F32), jax.ShapeDtypeStruct((m, n), BF16)]
    est = 2 * _nbytes((tm, d), BF16) + 2 * _nbytes((d, tn), BF16) + 2 * _nbytes((tm, FNET_W), F32)
    est += 2 * _nbytes((tm, tn), BF16) + 2 * _nbytes((tm, tn), F32) + 6 * _nbytes((tm, HEAD_DIM), F32)
    if caches:
        seqs = tm // seq
        for shp in cache_shapes:
            out_specs.append(pl.BlockSpec((seqs, None, seq, shp[3]), lambda i, j: (i, layer, 0, 0)))
            out_shape.append(jax.ShapeDtypeStruct(shp, F32))
            est += 2 * _nbytes((tm, shp[3]), F32)
    return pl.pallas_call(
        functools.partial(_proj_kernel, rope=rope, caches=caches, n_alias=n_alias),
        grid=(m // tm, n // tn),
        in_specs=in_specs,
        out_specs=out_specs,
        out_shape=out_shape,
        input_output_aliases=aliases,
        compiler_params=_cparams(("arbitrary", "arbitrary"), est),
        name="in_proj",
    )(*args)


def _sigmoid(z):
    return 1.0 / (1.0 + jnp.exp(-z))


def _side_cast_operands(side_cast, steps, step_index):
    stack, src_layer = side_cast
    depth, src_k, src_n = stack.shape
    rows = src_k // steps
    in_spec = pl.BlockSpec((None, None, rows, src_n), lambda *ids: (src_layer, step_index(*ids), 0, 0))
    out_spec = pl.BlockSpec((None, rows, src_n), lambda *ids: (step_index(*ids), 0, 0))
    vmem = 3 * _nbytes((rows, src_n), F32) + 2 * _nbytes((rows, src_n), BF16)
    return (in_spec, stack.reshape(depth, steps, rows, src_n), out_spec,
            jax.ShapeDtypeStruct((steps, rows, src_n), BF16), vmem, (src_k, src_n))


def _merge_kernel(xg_ref, ssq_ref, ya_ref, yb_ref, yc_ref, wga_ref, wgb_ref, wgc_ref, bga_ref, bgb_ref, bgc_ref,
                  sga_ref, sgb_ref, sgc_ref, wa_ref, wb_ref, wc_ref, o_ref):
    xg = xg_ref[...]
    r = _row_scale(ssq_ref, xg.shape[1])

    def branch(wg_ref, bg_ref, sg_ref, y_ref, w_ref):
        pre = r * jnp.dot(xg, wg_ref[...], preferred_element_type=F32) + (sg_ref[...] + bg_ref[...])
        return _sigmoid(pre) * jnp.dot(y_ref[...], w_ref[...], preferred_element_type=F32)

    m = branch(wga_ref, bga_ref, sga_ref, ya_ref, wa_ref)
    m = m + branch(wgb_ref, bgb_ref, sgb_ref, yb_ref, wb_ref)
    m = m + branch(wgc_ref, bgc_ref, sgc_ref, yc_ref, wc_ref)
    o_ref[...] = m.astype(o_ref.dtype)


def merge(xg, ssq, ya, yb, yc, w_gate, b_gate, shw_gate, w_a, w_b, w_c, layer, mod, *, tm=512, tn=512):
    m, d = xg.shape
    nj = d // tn
    b_gate = b_gate.reshape(b_gate.shape[0], 1, 3 * d)
    row = lambda i, j: (i, 0)

    def stack_spec(rows, part):
        return pl.BlockSpec((None, rows, tn), lambda i, j: (layer, 0, part * nj + j))

    def gate_spec(part):
        return _weight_spec(w_gate, layer if w_gate.ndim == 3 else None, d, tn, lambda i, j: (0, part * nj + j))

    def shw_spec(part):
        return pl.BlockSpec((None, 1, tn), lambda i, j: (mod.index(i, tm), 0, part * nj + j))

    in_specs = [
        pl.BlockSpec((tm, d), row),
        pl.BlockSpec((tm, SSQ_LANES), row),
        pl.BlockSpec((tm, ya.shape[1]), row),
        pl.BlockSpec((tm, yb.shape[1]), row),
        pl.BlockSpec((tm, yc.shape[1]), row),
        gate_spec(0), gate_spec(1), gate_spec(2),
        stack_spec(1, 0), stack_spec(1, 1), stack_spec(1, 2),
        shw_spec(0), shw_spec(1), shw_spec(2),
        stack_spec(w_a.shape[1], 0), stack_spec(w_b.shape[1], 0), stack_spec(w_c.shape[1], 0),
    ]
    est = 2 * 2 * _nbytes((tm, d), BF16) + 2 * 4 * _nbytes((d, tn), BF16) + 8 * _nbytes((tm, tn), F32)
    return pl.pallas_call(
        _merge_kernel,
        grid=(m // tm, nj),
        in_specs=in_specs,
        out_specs=pl.BlockSpec((tm, tn), lambda i, j: (i, j)),
        out_shape=jax.ShapeDtypeStruct((m, d), BF16),
        compiler_params=_cparams(("arbitrary", "arbitrary"), est),
        name="merge",
    )(xg, ssq, ya, yb, yc, w_gate, w_gate, w_gate, b_gate, b_gate, b_gate, shw_gate, shw_gate, shw_gate,
      w_a, w_b, w_c)


def _split_hi_lo(x):
    hi = x.astype(BF16)
    lo = (x - hi.astype(F32)).astype(BF16)
    return hi, lo


def _dot3(xh, xl, wh, wl):
    acc = jnp.dot(xh, wh, preferred_element_type=F32)
    acc += jnp.dot(xl, wh, preferred_element_type=F32)
    acc += jnp.dot(xh, wl, preferred_element_type=F32)
    return acc


def _dot_nt(a, b):
    return lax.dot_general(a, b, (((1,), (1,)), ((), ())), preferred_element_type=F32)


def _row_max(scores, extra=None):
    m = scores[0].max(axis=-1, keepdims=True)
    for s in scores[1:]:
        m = jnp.maximum(m, s.max(axis=-1, keepdims=True))
    if extra is not None:
        m = jnp.maximum(m, extra)
    return m


def _sink_column(sink_ref, hk, rows_per_head):
    rows = WIN_GROUP * rows_per_head
    group = lax.broadcasted_iota(jnp.int32, (rows, 1), 0) // rows_per_head
    sink = jnp.zeros((rows, 1), F32)
    for g in range(WIN_GROUP):
        sink = jnp.where(group == g, sink_ref[hk * WIN_GROUP + g], sink)
    return sink


def _stacked_q(q_refs, hk):
    heads_per_ref = q_refs[0].shape[1] // HEAD_DIM
    parts = []
    for g in range(WIN_GROUP):
        h = hk * WIN_GROUP + g
        ref = q_refs[h // heads_per_ref]
        lo = (h % heads_per_ref) * HEAD_DIM
        parts.append(ref[:, lo:lo + HEAD_DIM])
    return jnp.concatenate(parts, axis=0)


def _ctx_mixer_kernel(a_ref, bq0_ref, bq1_ref, bk_ref, bv_ref, cq_ref, ck_ref, cv_ref, sink_ref,
                      ch_h_ref, ch_l_ref, pc_h_ref, pc_l_ref, ps_h_ref, ps_l_ref,
                      ya_ref, yb_ref, yc_ref, *, seq):
    gw = FNET_GROUP_W
    ch_h = ch_h_ref[...]
    ch_l = ch_l_ref[...]
    pc_h, pc_l, ps_h, ps_l = pc_h_ref[...], pc_l_ref[...], ps_h_ref[...], ps_l_ref[...]
    norm = 1.0 / math.sqrt(seq * gw)
    for g in range(FNET_W // gw):
        a_h, a_l = _split_hi_lo(a_ref[:, g * gw:(g + 1) * gw])
        t = _dot3(a_h, a_l, ch_h, ch_l)
        tc_h, tc_l = _split_hi_lo(t[:, :gw])
        ts_h, ts_l = _split_hi_lo(t[:, gw:])
        y = _dot3(pc_h, pc_l, tc_h, tc_l) - _dot3(ps_h, ps_l, ts_h, ts_l)
        ya_ref[:, g * gw:(g + 1) * gw] = (y * norm).astype(ya_ref.dtype)

    hd = HEAD_DIM
    for hk in range(WIN_KV_HEADS):
        q = _stacked_q((bq0_ref, bq1_ref), hk)
        s = _dot_nt(q, bk_ref[:, hk * hd:(hk + 1) * hd])
        sink = _sink_column(sink_ref, hk, seq)
        m = _row_max([s], sink)
        e = jnp.exp(s - m)
        den = e.sum(axis=-1, keepdims=True) + jnp.exp(sink - m)
        o = jnp.dot(e.astype(BF16), bv_ref[:, hk * hd:(hk + 1) * hd], preferred_element_type=F32) / den
        for g in range(WIN_GROUP):
            h = hk * WIN_GROUP + g
            yb_ref[:, h * hd:(h + 1) * hd] = o[g * seq:(g + 1) * seq].astype(yb_ref.dtype)

    for h in range(NA_HEADS):
        hs = slice(h * hd, (h + 1) * hd)
        s = _dot_nt(cq_ref[:, hs], ck_ref[:, hs])
        m = s.max(axis=-1, keepdims=True)
        e = jnp.exp(s - m)
        den = e.sum(axis=-1, keepdims=True)
        o = jnp.dot(e.astype(BF16), cv_ref[:, hs], preferred_element_type=F32) / den
        yc_ref[:, hs] = o.astype(yc_ref.dtype)


def ctx_mixers(a32, p16, sink, tabs, *, batch, seq):
    def col(width, offset):
        return pl.BlockSpec((seq, width), lambda b: (b, offset // width))

    const = lambda shape: pl.BlockSpec(shape, lambda b: (0,) * len(shape))
    half_q = WIN_Q_W // 2
    in_specs = [
        col(FNET_W, 0), col(half_q, COL_BQ), col(half_q, COL_BQ + half_q),
        col(WIN_KV_W, COL_BK), col(WIN_KV_W, COL_BV),
        col(NA_W, COL_CQ), col(NA_W, COL_CK), col(NA_W, COL_CV),
        pl.BlockSpec(memory_space=pltpu.SMEM),
        const(tabs["ch_h"].shape), const(tabs["ch_l"].shape),
        const(tabs["pc_h"].shape), const(tabs["pc_l"].shape),
        const(tabs["ps_h"].shape), const(tabs["ps_l"].shape),
    ]
    out_specs = [pl.BlockSpec((seq, FNET_W), lambda b: (b, 0)),
                 pl.BlockSpec((seq, WIN_Q_W), lambda b: (b, 0)),
                 pl.BlockSpec((seq, NA_W), lambda b: (b, 0))]
    m = batch * seq
    out_shape = [jax.ShapeDtypeStruct((m, FNET_W), BF16), jax.ShapeDtypeStruct((m, WIN_Q_W), BF16),
                 jax.ShapeDtypeStruct((m, NA_W), BF16)]
    est = 2 * _nbytes((seq, IN_W), F32) + 2 * _nbytes((seq, D_MODEL), BF16) + (16 << 20)
    return pl.pallas_call(
        functools.partial(_ctx_mixer_kernel, seq=seq),
        grid=(batch,),
        in_specs=in_specs,
        out_specs=out_specs,
        out_shape=out_shape,
        compiler_params=_cparams(("arbitrary",), est),
        name="ctx_mixers",
    )(a32, p16, p16, p16, p16, p16, p16, p16, sink,
      tabs["ch_h"], tabs["ch_l"], tabs["pc_h"], tabs["pc_l"], tabs["ps_h"], tabs["ps_l"])


def _group_dot3(x, tab_h, tab_l):
    gw = FNET_GROUP_W
    outs = []
    for g in range(FNET_W // gw):
        x_h, x_l = _split_hi_lo(x[:, g * gw:(g + 1) * gw])
        outs.append(_dot3(x_h, x_l, tab_h, tab_l))
    return jnp.concatenate(outs, axis=1)


def _shifted_reverse(jrev_ref, x_hi, x_lo):
    out = jnp.dot(jrev_ref[...], x_hi, preferred_element_type=F32)
    if x_lo is not None:
        out += jnp.dot(jrev_ref[...], x_lo, preferred_element_type=F32)
    return out


def _dft_fold_kernel(a_ref, am_ref, nb_ref, mid_ref, jrev_ref, ch_h_ref, ch_l_ref, te_ref, to_ref, aux_ref):
    i = pl.program_id(1)
    gw = FNET_GROUP_W
    cc_h, cc_l = ch_h_ref[:, :gw], ch_l_ref[:, :gw]
    sc_h, sc_l = ch_h_ref[:, gw:], ch_l_ref[:, gw:]
    a = a_ref[...]
    tm = a.shape[0]
    row = lax.broadcasted_iota(jnp.int32, (tm, 1), 0)
    am_h, am_l = _split_hi_lo(am_ref[...])
    first = jnp.where(i > 0, nb_ref[0:1, :], 0.0)
    arev = jnp.where(row == 0, first, _shifted_reverse(jrev_ref, am_h, am_l))
    e = a + arev
    o = jnp.where((row == 0) & (i == 0), 0.0, a - arev)

    te_h, te_l = _split_hi_lo(_group_dot3(e, cc_h, cc_l))
    to_h, to_l = _split_hi_lo(_group_dot3(o, sc_h, sc_l))
    te_ref[0], te_ref[1], te_ref[2] = te_h, te_h, te_l
    to_ref[0], to_ref[1], to_ref[2] = to_h, to_h, to_l

    @pl.when(i == 0)
    def _():
        rows8 = lax.broadcasted_iota(jnp.int32, aux_ref.shape, 0)
        aux_ref[...] = _group_dot3(jnp.where(rows8 == 0, mid_ref[0:1, :], 0.0), cc_h, cc_l)


def dft_fold(a32, tabs, *, batch, seq, tm=DFT_FOLD_TM):
    half = seq // 2
    nt = half // tm
    tiles = seq // tm
    sub = 8
    w = FNET_W
    slab = pl.BlockSpec((3, tm, w), lambda b, i: (0, i, b))
    const = lambda arr: pl.BlockSpec(arr.shape, lambda b, i: (0, 0))
    jrev = tabs["jrev_fold"]
    est = 6 * _nbytes((tm, w), F32) + 4 * 3 * _nbytes((tm, w), BF16) + 2 * _nbytes(jrev.shape, BF16) + (12 << 20)
    return pl.pallas_call(
        _dft_fold_kernel,
        grid=(batch, nt),
        in_specs=[pl.BlockSpec((tm, w), lambda b, i: (b * tiles + i, 0)),
                  pl.BlockSpec((tm, w), lambda b, i: (b * tiles + tiles - 1 - i, 0)),
                  pl.BlockSpec((sub, w), lambda b, i: ((b * seq + (seq - i * tm) % seq) // sub, 0)),
                  pl.BlockSpec((sub, w), lambda b, i: ((b * seq + half) // sub, 0)),
                  const(jrev), const(tabs["ch_h"]), const(tabs["ch_l"])],
        out_specs=[slab, slab, pl.BlockSpec((None, sub, w), lambda b, i: (b, 0, 0))],
        out_shape=[jax.ShapeDtypeStruct((3, half, batch * w), BF16),
                   jax.ShapeDtypeStruct((3, half, batch * w), BF16),
                   jax.ShapeDtypeStruct((batch, sub, w), F32)],
        compiler_params=_cparams(("arbitrary", "arbitrary"), est),
        name="dft_fold",
    )(a32, a32, a32, a32, jrev, tabs["ch_h"], tabs["ch_l"])


def _dft_pos_kernel(c3_ref, s3_ref, c3x_ref, s3x_ref, te_ref, to_ref, aux_ref, jrev_ref, lo_ref, hi_ref,
                    accc_ref, accs_ref, accx_ref, *, nk, norm):
    k = pl.program_id(2)

    def dots():
        te, to = te_ref[...], to_ref[...]
        return (jnp.dot(c3_ref[...], te, preferred_element_type=F32),
                jnp.dot(s3_ref[...], to, preferred_element_type=F32),
                jnp.dot(c3x_ref[...], te, preferred_element_type=F32)
                + jnp.dot(s3x_ref[...], to, preferred_element_type=F32))

    @pl.when(k == 0)
    def _():
        accc_ref[...], accs_ref[...], accx_ref[...] = dots()

    @pl.when((k > 0) & (k < nk - 1))
    def _():
        dc, ds, dx = dots()
        accc_ref[...] += dc
        accs_ref[...] += ds
        accx_ref[...] += dx

    @pl.when(k == nk - 1)
    def _():
        dc, ds, dx = dots()
        tm = dc.shape[0]
        row = lax.broadcasted_iota(jnp.int32, (tm, 1), 0)
        mid_c = aux_ref[0:1, :]
        pc = accc_ref[...] + dc + jnp.where(row % 2 == 0, mid_c, -mid_c)
        ps = accs_ref[...] + ds
        lo_ref[...] = ((pc - ps) * norm).astype(lo_ref.dtype)
        mirrored = _shifted_reverse(jrev_ref, ((pc + ps) * norm).astype(BF16), None)
        edge = ((accx_ref[...] + dx)[0:1, :] + mid_c) * norm
        hi_ref[...] = jnp.where(row == 0, edge, mirrored).astype(hi_ref.dtype)


def dft_pos(te, to, aux, tabs, *, batch, seq, tm=DFT_POS_TM, tk=1024):
    half = seq // 2
    w = FNET_W
    kdim = 3 * half
    nk = kdim // tk
    ni = half // tm
    sub = 8
    norm = 1.0 / math.sqrt(seq * FNET_GROUP_W)
    jrev = tabs["jrev_pos"]
    tab_spec = pl.BlockSpec((tm, tk), lambda i, j, k: (i, k))
    edge_spec = pl.BlockSpec((None, sub, tk), lambda i, j, k: (i, 0, k))
    slab_spec = pl.BlockSpec((tk, w), lambda i, j, k: (k, j))
    est = 4 * _nbytes((tm, tk), BF16) + 4 * _nbytes((tk, w), BF16) + 5 * _nbytes((tm, w), F32)
    est += 4 * _nbytes((tm, w), BF16) + 2 * _nbytes(jrev.shape, BF16) + (2 << 20)
    return pl.pallas_call(
        functools.partial(_dft_pos_kernel, nk=nk, norm=norm),
        grid=(ni, batch, nk),
        in_specs=[tab_spec, tab_spec, edge_spec, edge_spec, slab_spec, slab_spec,
                  pl.BlockSpec((None,) + aux.shape[1:], lambda i, j, k: (j, 0, 0)),
                  pl.BlockSpec(jrev.shape, lambda i, j, k: (0, 0))],
        out_specs=[pl.BlockSpec((None, tm, w), lambda i, j, k: (j, i, 0)),
                   pl.BlockSpec((None, tm, w), lambda i, j, k: (j, ni - 1 - i, 0))],
        out_shape=[jax.ShapeDtypeStruct((batch, half, w), BF16)] * 2,
        scratch_shapes=[pltpu.VMEM((tm, w), F32), pltpu.VMEM((tm, w), F32), pltpu.VMEM((sub, w), F32)],
        compiler_params=_cparams(("arbitrary", "arbitrary", "arbitrary"), est),
        name="dft_pos",
    )(tabs["c3"], tabs["s3"], tabs["c3x"], tabs["s3x"],
      te.reshape(kdim, batch * w), to.reshape(kdim, batch * w), aux, jrev)


def latent_fourier(a32, tabs, *, batch, seq):
    te, to, aux = dft_fold(a32, tabs, batch=batch, seq=seq)
    lo, hi = dft_pos(te, to, aux, tabs, batch=batch, seq=seq)
    return jnp.concatenate([lo, hi], axis=1).reshape(batch * seq, FNET_W)


def _win_attn_kernel(q0_ref, q1_ref, k_ref, v_ref, ck_ref, cv_ref, sink_ref, o_ref, *, seq):
    n = pl.program_id(1)
    blk = WIN
    span = 3 * blk
    hd = HEAD_DIM
    start = pl.multiple_of(jnp.clip((n - 1) * blk, 0, seq - span), blk)
    rows = WIN_GROUP * blk
    ipos = n * blk + lax.broadcasted_iota(jnp.int32, (rows, span), 0) % blk
    jpos = start + lax.broadcasted_iota(jnp.int32, (rows, span), 1)
    valid = jnp.abs(ipos - jpos) <= WIN

    for hk in range(WIN_KV_HEADS):
        hs = slice(hk * hd, (hk + 1) * hd)
        q = _stacked_q((q0_ref, q1_ref), hk)
        s = jnp.where(valid, _dot_nt(q, k_ref[pl.ds(start, span), hs]), NEG_INF)
        sc = _dot_nt(q, ck_ref[:, hs].astype(BF16))
        sink = _sink_column(sink_ref, hk, blk)
        m = _row_max([s, sc], sink)
        e = jnp.exp(s - m)
        ec = jnp.exp(sc - m)
        den = e.sum(axis=-1, keepdims=True) + ec.sum(axis=-1, keepdims=True) + jnp.exp(sink - m)
        o = jnp.dot(e.astype(BF16), v_ref[pl.ds(start, span), hs], preferred_element_type=F32)
        o += jnp.dot(ec.astype(BF16), cv_ref[:, hs].astype(BF16), preferred_element_type=F32)
        o = o / den
        for g in range(WIN_GROUP):
            h = hk * WIN_GROUP + g
            o_ref[:, h * hd:(h + 1) * hd] = o[g * blk:(g + 1) * blk].astype(o_ref.dtype)


def win_attention(p16, cache_k, cache_v, sink, *, batch, seq, layer):
    blk = WIN
    nblk = seq // blk
    past = cache_k.shape[2]
    half_q = WIN_Q_W // 2

    def q_spec(offset):
        return pl.BlockSpec((blk, half_q), lambda b, n: (b * nblk + n, offset // half_q))

    def seq_spec(offset):
        return pl.BlockSpec((seq, WIN_KV_W), lambda b, n: (b, offset // WIN_KV_W))

    cache_spec = pl.BlockSpec((None, None, past, WIN_KV_W), lambda b, n: (b, layer, 0, 0))
    in_specs = [q_spec(COL_BQ), q_spec(COL_BQ + half_q), seq_spec(COL_BK), seq_spec(COL_BV),
                cache_spec, cache_spec, pl.BlockSpec(memory_space=pltpu.SMEM)]
    est = 4 * _nbytes((seq, WIN_KV_W), BF16) + 4 * _nbytes((past, WIN_KV_W), F32) + (16 << 20)
    return pl.pallas_call(
        functools.partial(_win_attn_kernel, seq=seq),
        grid=(batch, nblk),
        in_specs=in_specs,
        out_specs=pl.BlockSpec((blk, WIN_Q_W), lambda b, n: (b * nblk + n, 0)),
        out_shape=jax.ShapeDtypeStruct((batch * seq, WIN_Q_W), BF16),
        compiler_params=_cparams(("arbitrary", "arbitrary"), est),
        name="win_attn",
    )(p16, p16, p16, p16, cache_k, cache_v, sink)


def _na_window_start(rg, rows_n):
    return jnp.clip(rg * NA_QROWS - NA_KH // 2, 0, rows_n - NA_KROWS)


def _na_attn_kernel(q_ref, k_ref, v_ref, ck_ref, cv_ref, bias_ref, o_ref, *, rows_n):
    rg = pl.program_id(1)
    hd = HEAD_DIM
    nkeys = NA_KROWS * GRID_W
    start = pl.multiple_of(_na_window_start(rg, rows_n) * GRID_W, GRID_W)
    for h in range(NA_HEADS):
        hs = slice(h * hd, (h + 1) * hd)
        q = q_ref[:, hs]
        s = _dot_nt(q, k_ref[pl.ds(start, nkeys), hs]) + bias_ref[h]
        sc = _dot_nt(q, ck_ref[:, hs].astype(BF16))
        m = _row_max([s, sc])
        e = jnp.exp(s - m)
        ec = jnp.exp(sc - m)
        den = e.sum(axis=-1, keepdims=True) + ec.sum(axis=-1, keepdims=True)
        o = jnp.dot(e.astype(BF16), v_ref[pl.ds(start, nkeys), hs], preferred_element_type=F32)
        o += jnp.dot(ec.astype(BF16), cv_ref[:, hs].astype(BF16), preferred_element_type=F32)
        o_ref[:, hs] = (o / den).astype(o_ref.dtype)


def na_attention(p16, cache_k, cache_v, bias_tab, *, batch, seq, layer):
    rows_n = seq // GRID_W
    ngroups = rows_n // NA_QROWS
    nq = NA_QROWS * GRID_W
    nkeys = NA_KROWS * GRID_W
    past = cache_k.shape[2]

    def seq_spec(offset):
        return pl.BlockSpec((seq, NA_W), lambda b, rg: (b, offset // NA_W), pipeline_mode=pl.Buffered(1))

    def pattern(b, rg):
        return (layer, (rg > 0).astype(jnp.int32) + (rg == ngroups - 1).astype(jnp.int32), 0, 0, 0)

    cache_spec = pl.BlockSpec((None, None, past, NA_W), lambda b, rg: (b, layer, 0, 0))
    in_specs = [pl.BlockSpec((nq, NA_W), lambda b, rg: (b * ngroups + rg, COL_CQ // NA_W)),
                seq_spec(COL_CK), seq_spec(COL_CV), cache_spec, cache_spec,
                pl.BlockSpec((None, None, NA_HEADS, nq, nkeys), pattern, pipeline_mode=pl.Buffered(1))]
    est = 2 * _nbytes((seq, NA_W), BF16) + _nbytes((NA_HEADS, nq, nkeys), F32)
    est += 4 * _nbytes((past, NA_W), F32) + (12 << 20)
    return pl.pallas_call(
        functools.partial(_na_attn_kernel, rows_n=rows_n),
        grid=(batch, ngroups),
        in_specs=in_specs,
        out_specs=pl.BlockSpec((nq, NA_W), lambda b, rg: (b * ngroups + rg, 0)),
        out_shape=jax.ShapeDtypeStruct((batch * seq, NA_W), BF16),
        compiler_params=_cparams(("arbitrary", "arbitrary"), est),
        name="na_attn",
    )(p16, p16, p16, cache_k, cache_v, bias_tab)


def _dft_cos_sin(n):
    idx = jnp.arange(n, dtype=jnp.int32)
    phase = (idx[:, None] * idx[None, :]) % n
    ang = phase.astype(F32) * (2.0 * math.pi / n)
    return jnp.cos(ang), jnp.sin(ang)


def _hi_lo(x):
    hi = x.astype(BF16)
    return hi, (x - hi.astype(F32)).astype(BF16)


def _fourier_tables(ctx_seq, lat_seq):
    cc, sc = _dft_cos_sin(FNET_GROUP_W)
    ch_h, ch_l = _hi_lo(jnp.concatenate([cc, sc], axis=1))
    cp, sp = _dft_cos_sin(ctx_seq)
    pc_h, pc_l = _hi_lo(cp)
    ps_h, ps_l = _hi_lo(sp)
    half = lat_seq // 2
    idx = jnp.arange(half, dtype=jnp.int32)

    def tables(rows):
        ang = ((rows[:, None] * idx[None, :]) % lat_seq).astype(F32) * (2.0 * math.pi / lat_seq)
        c_h, c_l = _hi_lo(jnp.cos(ang))
        s_h, s_l = _hi_lo(jnp.sin(ang))
        return jnp.concatenate([c_h, c_l, c_h], axis=1), jnp.concatenate([s_h, s_l, s_h], axis=1)

    c3, s3 = tables(idx)
    ni = half // DFT_POS_TM
    c3x, s3x = tables((jnp.arange(ni, dtype=jnp.int32) + 1) * DFT_POS_TM)
    pad = lambda t: jnp.pad(t[:, None, :], ((0, 0), (0, 7), (0, 0)))
    return dict(ch_h=ch_h, ch_l=ch_l, pc_h=pc_h, pc_l=pc_l, ps_h=ps_h, ps_l=ps_l, c3=c3, s3=s3,
                c3x=pad(c3x), s3x=pad(s3x),
                jrev_fold=_shifted_antidiagonal(DFT_FOLD_TM), jrev_pos=_shifted_antidiagonal(DFT_POS_TM))


def _shifted_antidiagonal(n):
    r = jnp.arange(n)
    return ((r[:, None] + r[None, :]) == n).astype(BF16)


def _rope_tables(seq):
    t = jnp.arange(seq)
    n_freq = HEAD_DIM // 4
    inv = ROPE_BASE ** (-jnp.arange(n_freq, dtype=F32) / n_freq)
    ang_r = (t // GRID_W).astype(F32)[:, None] * inv
    ang_c = (t % GRID_W).astype(F32)[:, None] * inv
    zeros = jnp.zeros_like(ang_r)
    cos = jnp.concatenate([jnp.cos(ang_r)] * 2 + [jnp.cos(ang_c)] * 2, axis=1)
    sin_lo = jnp.concatenate([-jnp.sin(ang_r), zeros, -jnp.sin(ang_c), zeros], axis=1)
    sin_hi = jnp.concatenate([zeros, jnp.sin(ang_r), zeros, jnp.sin(ang_c)], axis=1)
    return cos, sin_lo, sin_hi


def _na_bias_tables(rpb, rows_n):
    depth = rpb.shape[0]
    col = jnp.arange(GRID_W)
    cs = jnp.clip(col - NA_KW // 2, 0, GRID_W - NA_KW)
    col_ok = (col[None, :] >= cs[:, None]) & (col[None, :] < cs[:, None] + NA_KW)
    idx_c = jnp.clip(col[None, :] - col[:, None], -(NA_KW - 1), NA_KW - 1) + NA_KW - 1
    bt = jnp.where(col_ok[None, None, None], rpb.astype(F32)[:, :, :, idx_c], NEG_INF)

    ngroups = rows_n // NA_QROWS
    groups = jnp.array([0, 1, ngroups - 1])
    r = groups[:, None] * NA_QROWS + jnp.arange(NA_QROWS)[None, :]
    ws = jnp.clip(groups * NA_QROWS - NA_KH // 2, 0, rows_n - NA_KROWS)
    krow = ws[:, None] + jnp.arange(NA_KROWS)[None, :]
    rs = jnp.clip(r - NA_KH // 2, 0, rows_n - NA_KH)
    row_ok = (krow[:, None, :] >= rs[:, :, None]) & (krow[:, None, :] < rs[:, :, None] + NA_KH)
    rho = jnp.clip(krow[:, None, :] - r[:, :, None] + NA_KH - 1, 0, 2 * NA_KH - 2)
    tab = jnp.where(row_ok[None, None, :, :, :, None, None], bt[:, :, rho], NEG_INF)
    tab = jnp.transpose(tab, (0, 2, 1, 3, 5, 4, 6))
    return tab.reshape(depth, 3, NA_HEADS, NA_QROWS * GRID_W, NA_KROWS * GRID_W)


def kernel(x_prompt, x_sample, cache_win_k, cache_win_v, cache_na_k, cache_na_v, c, c_ctx, w_mod, b_mod,
           norm1_g, w_in, win_sink, na_rpb, w_a, w_b, w_c, w_gate, b_gate, w_out, norm2_g, w1, w2, final_g):
    batch, seq, d = x_prompt.shape
    dec_batch, dec_seq, _ = x_sample.shape
    past = cache_win_k.shape[2]
    m_ctx, m_lat = batch * seq, dec_batch * dec_seq

    cvec = jnp.concatenate([c_ctx[None], c, jnp.zeros((N_MOD - 1 - dec_batch, d), F32)], axis=0)
    mods = adaln_all(cvec, w_mod, b_mod).reshape(DEPTH, N_MOD, 6, 1, d)
    shifts = jnp.stack([mods[:, :, 0, 0], mods[:, :, 3, 0]], axis=0)
    shifts = jnp.pad(shifts, ((0, 0), (0, 0), (0, MOD_PAD_ROWS - N_MOD), (0, 0))).astype(BF16)

    w_in0, sh1_w_in = cast_bias(w_in, shifts[0], layers=1)
    w_gate0, sh1_w_gate = cast_bias(w_gate, shifts[0], layers=1)
    w_in_cur, w_gate_cur, sh1_w_in, sh1_w_gate = w_in0[0], w_gate0[0], sh1_w_in[0], sh1_w_gate[0]
    w1_cur, w2_cur = w1[0].astype(BF16), w2[0].astype(BF16)
    w_a_b, w_b_b, w_c_b, w_out_b = w_a.astype(BF16), w_b.astype(BF16), w_c.astype(BF16), w_out.astype(BF16)

    ftabs = _fourier_tables(seq, dec_seq)
    rope_tabs = _rope_tables(dec_seq)
    na_bias = _na_bias_tables(na_rpb, dec_seq // GRID_W)
    cwk, cwv = (t.reshape(dec_batch, DEPTH, past, WIN_KV_W) for t in (cache_win_k, cache_win_v))
    cnk, cnv = (t.reshape(dec_batch, DEPTH, past, NA_W) for t in (cache_na_k, cache_na_v))
    cache_shapes = [(batch, DEPTH, seq, WIN_KV_W)] * 2 + [(batch, DEPTH, seq, NA_W)] * 2

    group_mods = (ModRows(0, m_ctx), ModRows(1, dec_seq))
    xs_by_group = [x_prompt.reshape(m_ctx, d), x_sample.reshape(m_lat, d)]
    pending = [prep_norm(x, norm1_g[0], mods[0, :, 1], mod) for x, mod in zip(xs_by_group, group_mods)]
    new_caches = [jnp.zeros(shape, F32) for shape in cache_shapes]
    for l in range(DEPTH):
        if l > 0:
            sh1_w_in = mod_bias(shifts[0, l], w_in_cur, None)
            sh1_w_gate = mod_bias(shifts[0, l], w_gate_cur, None)
        sh2_w1 = mod_bias(shifts[1, l], w1_cur, None)
        gate1, scale2, gate2 = mods[l, :, 2], mods[l, :, 4], mods[l, :, 5]
        last = l + 1 == DEPTH
        next_proj = [None, None]
        next_mlp = [None, None]
        for gi, mod in enumerate(group_mods):
            x = xs_by_group[gi]
            xg, ssq = pending[gi]
            if gi == 0:
                a32, p16, *new_caches = in_projection(xg, ssq, w_in_cur, sh1_w_in, l, mod, seq=seq,
                                                      cache_shapes=cache_shapes, prev_caches=new_caches, tm=512)
                ya, yb, yc = ctx_mixers(a32, p16, win_sink[l], ftabs, batch=batch, seq=seq)
            else:
                a32, p16 = in_projection(xg, ssq, w_in_cur, sh1_w_in, l, mod, rope_tabs, seq=dec_seq)
                ya = latent_fourier(a32, ftabs, batch=dec_batch, seq=dec_seq)
                yb = win_attention(p16, cwk, cwv, win_sink[l], batch=dec_batch, seq=dec_seq, layer=l)
                yc = na_attention(p16, cnk, cnv, na_bias, batch=dec_batch, seq=dec_seq, layer=l)
            mrg = merge(xg, ssq, ya, yb, yc, w_gate_cur, b_gate, sh1_w_gate, w_a_b, w_b_b, w_c_b, l, mod)
            x, xg2, ssq2 = matmul_resid(mrg, w_out_b, l, x, gate1, mod, next_norm=(norm2_g[l], scale2), tn=512)
            if last:
                u = mlp_up(xg2, ssq2, w1_cur, sh2_w1, None, mod)
                x = matmul_resid(u, w2_cur, None, x, gate2, mod, tk=2048)
            else:
                side_casts = [((w1, w2)[gi], l + 1), ((w_in, w_gate)[gi], l + 1)]
                u, (next_mlp[gi], next_proj[gi]) = mlp_up(xg2, ssq2, w1_cur, sh2_w1, None, mod,
                                                          side_casts=side_casts)
                x, xg, ssq = matmul_resid(u, w2_cur, None, x, gate2, mod, tk=2048,
                                          next_norm=(norm1_g[l + 1], mods[l + 1, :, 1]))
                pending[gi] = (xg, ssq)
            xs_by_group[gi] = x
        w_in_cur, w_gate_cur = next_proj
        w1_cur, w2_cur = next_mlp

    y_prompt = final_norm(xs_by_group[0], final_g).reshape(batch, seq, d)
    y_sample = final_norm(xs_by_group[1], final_g).reshape(dec_batch, dec_seq, d)
    wk, wv, nk, nv = new_caches
    kv_shape = lambda heads: (batch, DEPTH, seq, heads, HEAD_DIM)
    return (y_prompt, y_sample, wk.reshape(kv_shape(WIN_KV_HEADS)), wv.reshape(kv_shape(WIN_KV_HEADS)),
            nk.reshape(kv_shape(NA_HEADS)), nv.reshape(kv_shape(NA_HEADS)))
```
